```python
import math
import jax
import jax.numpy as jnp
from jax import lax
import numpy as np

D_MODEL = 1024
BATCH = 8
SEQ = 2048
DEPTH = 4
DEC_BATCH = 128
DEC_SEQ = 1
PAST_LEN = 2048
PAGE_SIZE = 128

MIX_W = D_MODEL
ATT_W = MIX_W // 2
ATT_HEAD_DIM = 64
ATT_HEADS = ATT_W // ATT_HEAD_DIM
SSM_W = MIX_W - ATT_W
SSM_HEAD_DIM = 64
SSM_HEADS = SSM_W // SSM_HEAD_DIM
SSM_STATE = 128
SSM_GROUPS = 2
CONV_W = 4
CONV_DIM = SSM_W + 2 * SSM_GROUPS * SSM_STATE
IN_DIM = 3 * ATT_W + SSM_W + CONV_DIM + SSM_HEADS
Q_BLOCK = 128
SSD_CHUNK = 128
ATT_BIAS_INIT = -5.0
N_EXPERTS = 16
N_EXPERT_GROUPS = 4
EXPERTS_PER_GROUP = N_EXPERTS // N_EXPERT_GROUPS
TOP_K = 2
D_FF_EXPERT = D_MODEL // 2
DN_ALPHA = (2 * DEPTH) ** 0.25
DN_BETA = (8 * DEPTH) ** -0.25
NORM_EPS = 1e-5

kernel_name = 'hymba_stickbreaking_ssd_groupmoe_decode_step'


def layer_norm(x, g, b):
    xf = x.astype(jnp.float32)
    mu = xf.mean(-1, keepdims=True)
    var = jnp.square(xf - mu).mean(-1, keepdims=True)
    return ((xf - mu) * lax.rsqrt(var + NORM_EPS) * g + b).astype(x.dtype)


def rms_norm_f32(x, g):
    xf = x.astype(jnp.float32)
    return xf * lax.rsqrt(jnp.square(xf).mean(-1, keepdims=True) + NORM_EPS) * g


def split_proj(h, w_in_l):
    p = h @ w_in_l
    cuts = [ATT_W, 2 * ATT_W, 3 * ATT_W, 3 * ATT_W + SSM_W, 3 * ATT_W + SSM_W + CONV_DIM]
    q, k, v, z, xbc, dt_raw = jnp.split(p, cuts, axis=-1)
    bsz, L = h.shape[:2]
    heads = lambda t: t.reshape(bsz, L, ATT_HEADS, ATT_HEAD_DIM)
    return heads(q), heads(k), heads(v), z, xbc, dt_raw


def stick_breaking(q, k, v, q_pos, k_pos, bias):
    z = jnp.einsum('bqhd,bkhd->bhqk', q, k).astype(jnp.float32) * (ATT_HEAD_DIM ** -0.5)
    z = z + bias.astype(jnp.float32)[None, :, None, None]
    mask = k_pos[None, :] < q_pos[:, None]
    log_beta = jax.nn.log_sigmoid(z)
    log_1m_beta = jnp.where(mask, log_beta - z, 0.0)
    suffix = lax.cumsum(log_1m_beta, axis=3, reverse=True) - log_1m_beta
    w = jnp.where(mask, jnp.exp(log_beta + suffix), 0.0)
    return jnp.einsum('bhqk,bkhd->bqhd', w.astype(v.dtype), v)


def stick_breaking_prompt(q, k, v, bias):
    bsz, L = q.shape[:2]
    nb = L // Q_BLOCK
    qb = q.reshape(bsz, nb, Q_BLOCK, ATT_HEADS, ATT_HEAD_DIM).swapaxes(0, 1)
    pos = jnp.arange(L)
    qpos = pos.reshape(nb, Q_BLOCK)
    out = lax.map(lambda a: stick_breaking(a[0], k, v, a[1], pos, bias), (qb, qpos))
    return out.swapaxes(0, 1).reshape(bsz, L, ATT_HEADS, ATT_HEAD_DIM)


def causal_conv(buf, w, b):
    c = buf.shape[-1]
    out = lax.conv_general_dilated(buf, w[:, None, :], window_strides=(1,), padding='VALID',
                                   dimension_numbers=('NWC', 'WIO', 'NWC'), feature_group_count=c)
    return out + b


def ssd_inputs(xbc_act, dt_raw, dt_bias_l, a_log_l):
    bsz, L = xbc_act.shape[:2]
    xs, bm, cm = jnp.split(xbc_act, [SSM_W, SSM_W + SSM_GROUPS * SSM_STATE], axis=-1)
    f32 = jnp.float32
    x = xs.reshape(bsz, L, SSM_HEADS, SSM_HEAD_DIM).astype(f32)
    rep = SSM_HEADS // SSM_GROUPS
    bm = jnp.repeat(bm.reshape(bsz, L, SSM_GROUPS, SSM_STATE), rep, axis=2).astype(f32)
    cm = jnp.repeat(cm.reshape(bsz, L, SSM_GROUPS, SSM_STATE), rep, axis=2).astype(f32)
    dt = jax.nn.softplus(dt_raw.astype(f32) + dt_bias_l.astype(f32))
    a = -jnp.exp(a_log_l.astype(f32))
    return x, dt, a, bm, cm


def segsum(x):
    T = x.shape[-1]
    xe = jnp.repeat(x[..., None], T, axis=-1)
    strict = jnp.tril(jnp.ones((T, T), dtype=bool), -1)
    cs = jnp.cumsum(jnp.where(strict, xe, 0.0), axis=-2)
    return jnp.where(jnp.tril(jnp.ones((T, T), dtype=bool), 0), cs, -jnp.inf)


def ssd_chunked(x, dt, a, bm, cm):
    bsz, L, H, P = x.shape
    N = bm.shape[-1]
    nc = L // SSD_CHUNK
    chunk = lambda t: t.reshape((bsz, nc, SSD_CHUNK) + t.shape[2:])
    xc, bc, cc = chunk(x * dt[..., None]), chunk(bm), chunk(cm)
    da = chunk(dt * a).transpose(0, 3, 1, 2)
    a_cum = jnp.cumsum(da, axis=-1)
    scores = jnp.einsum('bclhn,bcshn->bhcls', cc, bc) * jnp.exp(segsum(da))
    y_diag = jnp.einsum('bhcls,bcshp->bclhp', scores, xc)
    decay_to_end = jnp.exp(a_cum[..., -1:] - a_cum)
    chunk_states = jnp.einsum('bclhn,bhcl,bclhp->bchpn', bc, decay_to_end, xc)
    chunk_decay = jnp.exp(a_cum[..., -1])

    def step(h, inp):
        dec, s = inp
        return dec[..., None, None] * h + s, h

    h0 = jnp.zeros((bsz, H, P, N), jnp.float32)
    h_fin, h_prev = lax.scan(step, h0, (chunk_decay.transpose(2, 0, 1), chunk_states.swapaxes(0, 1)))
    h_prev = h_prev.swapaxes(0, 1)
    y_off = jnp.einsum('bclhn,bchpn->bclhp', cc, h_prev) * jnp.exp(a_cum).transpose(0, 2, 3, 1)[..., None]
    return (y_diag + y_off).reshape(bsz, L, H, P), h_fin


def ssd_recurrent(h0, x, dt, a, bm, cm):
    def step(h, inp):
        x_t, dt_t, b_t, c_t = inp
        h = jnp.exp(dt_t * a)[..., None, None] * h + jnp.einsum('bhp,bhn->bhpn', x_t * dt_t[..., None], b_t)
        return h, jnp.einsum('bhpn,bhn->bhp', h, c_t)

    sw = lambda t: t.swapaxes(0, 1)
    h, ys = lax.scan(step, h0.astype(jnp.float32), (sw(x), sw(dt), sw(bm), sw(cm)))
    return sw(ys), h


def merge_heads(att, y, x_ssm, z, d_skip_l, g_att, g_ssm, w_out_l):
    dtype = z.dtype
    bsz, L = att.shape[:2]
    att = rms_norm_f32(att.reshape(bsz, L, ATT_W), g_att)
    y = (y + x_ssm * d_skip_l.astype(jnp.float32)[:, None]).reshape(bsz, L, SSM_W)
    y = rms_norm_f32(y * jax.nn.silu(z.astype(jnp.float32)), g_ssm)
    return jnp.concatenate([att.astype(dtype), y.astype(dtype)], axis=-1) @ w_out_l


def moe(x, w_router, b_router, w_gate_l, w_up_l, w_down_l):
    shp = x.shape
    t = x.reshape(-1, D_MODEL)
    probs = jax.nn.softmax((t @ w_router + b_router).astype(jnp.float32), axis=-1)
    grouped = probs.reshape(-1, N_EXPERT_GROUPS, EXPERTS_PER_GROUP)
    g_sel = jnp.argmax(grouped.max(-1), axis=-1)
    in_group = jnp.einsum('tge,tg->te', grouped, jax.nn.one_hot(g_sel, N_EXPERT_GROUPS, dtype=jnp.float32))
    top_p, top_i = lax.top_k(in_group, TOP_K)
    gates = top_p / top_p.sum(-1, keepdims=True)
    idx = g_sel[:, None] * EXPERTS_PER_GROUP + top_i
    gate_full = jnp.einsum('tk,tke->te', gates, jax.nn.one_hot(idx, N_EXPERTS, dtype=jnp.float32)).astype(x.dtype)
    out = jnp.zeros_like(t)
    for e in range(N_EXPERTS):
        h = jax.nn.silu(t @ w_gate_l[e]) * (t @ w_up_l[e])
        out = out + gate_full[:, e:e + 1] * (h @ w_down_l[e])
    return out.reshape(shp)


def setup_inputs(seed: int = 0) -> dict:
    key = jax.random.key(seed)
    ks = jax.random.split(key, 26)
    f32 = jnp.float32
    n_pages = PAST_LEN // PAGE_SIZE
    n_used = DEC_BATCH * n_pages
    n_pool = (n_used * 5) // 4

    def nrm(k, shape, scale=1.0):
        return scale * jax.random.normal(k, shape, f32)

    x_prompt = nrm(ks[0], (BATCH, SEQ, D_MODEL))
    x_sample = nrm(ks[1], (DEC_BATCH, DEC_SEQ, D_MODEL))
    cache_k = nrm(ks[2], (DEPTH, n_pool, PAGE_SIZE, ATT_HEADS, ATT_HEAD_DIM))
    cache_v = nrm(ks[3], (DEPTH, n_pool, PAGE_SIZE, ATT_HEADS, ATT_HEAD_DIM))
    state_ssm = nrm(ks[4], (DEPTH, DEC_BATCH, SSM_HEADS, SSM_HEAD_DIM, SSM_STATE), 0.3)
    state_conv = nrm(ks[5], (DEPTH, DEC_BATCH, CONV_W - 1, CONV_DIM))
    page_table = jax.random.permutation(ks[6], n_pool)[:n_used].reshape(DEC_BATCH, n_pages).astype(jnp.int32)
    w_in = nrm(ks[7], (DEPTH, D_MODEL, IN_DIM), D_MODEL ** -0.5)
    att_bias = ATT_BIAS_INIT + nrm(ks[25], (DEPTH, ATT_HEADS), 0.1)
    conv_w = nrm(ks[8], (DEPTH, CONV_W, CONV_DIM), CONV_W ** -0.5)
    conv_b = nrm(ks[9], (DEPTH, CONV_DIM), 0.02)
    dt0 = jnp.exp(jax.random.uniform(ks[10], (DEPTH, SSM_HEADS), f32, math.log(1e-3), math.log(1e-1)))
    dt_bias = dt0 + jnp.log(-jnp.expm1(-dt0))
    a_log = jnp.log(jax.random.uniform(ks[11], (DEPTH, SSM_HEADS), f32, 1.0, 16.0))
    d_skip = 1.0 + nrm(ks[12], (DEPTH, SSM_HEADS), 0.1)
    attn_norm_g = 1.0 + nrm(ks[13], (DEPTH, ATT_W), 0.02)
    ssm_norm_g = 1.0 + nrm(ks[14], (DEPTH, SSM_W), 0.02)
    w_out = nrm(ks[15], (DEPTH, MIX_W, D_MODEL), DN_BETA * MIX_W ** -0.5)
    ln1_g = 1.0 + nrm(ks[16], (DEPTH, D_MODEL), 0.02)
    ln1_b = nrm(ks[17], (DEPTH, D_MODEL), 0.02)
    ln2_g = 1.0 + nrm(ks[18], (DEPTH, D_MODEL), 0.02)
    ln2_b = nrm(ks[19], (DEPTH, D_MODEL), 0.02)
    w_router = nrm(ks[20], (D_MODEL, N_EXPERTS), D_MODEL ** -0.5)
    b_router = nrm(ks[21], (N_EXPERTS,), 0.01)
    w_gate = nrm(ks[22], (DEPTH, N_EXPERTS, D_MODEL, D_FF_EXPERT), D_MODEL ** -0.5)
    w_up = nrm(ks[23], (DEPTH, N_EXPERTS, D_MODEL, D_FF_EXPERT), D_MODEL ** -0.5)
    w_down = nrm(ks[24], (DEPTH, N_EXPERTS, D_FF_EXPERT, D_MODEL), DN_BETA * D_FF_EXPERT ** -0.5)
    return {'x_prompt': x_prompt, 'x_sample': x_sample, 'cache_k': cache_k, 'cache_v': cache_v,
            'state_ssm': state_ssm, 'state_conv': state_conv, 'page_table': page_table,
            'w_in': w_in, 'att_bias': att_bias, 'conv_w': conv_w, 'conv_b': conv_b, 'dt_bias': dt_bias,
            'a_log': a_log, 'd_skip': d_skip, 'attn_norm_g': attn_norm_g, 'ssm_norm_g': ssm_norm_g,
            'w_out': w_out, 'ln1_g': ln1_g, 'ln1_b': ln1_b, 'ln2_g': ln2_g, 'ln2_b': ln2_b,
            'w_router': w_router, 'b_router': b_router, 'w_gate': w_gate, 'w_up': w_up, 'w_down': w_down}


def reference(x_prompt, x_sample, cache_k, cache_v, state_ssm, state_conv, page_table,
              w_in, att_bias, conv_w, conv_b, dt_bias, a_log, d_skip, attn_norm_g, ssm_norm_g, w_out,
              ln1_g, ln1_b, ln2_g, ln2_b, w_router, b_router, w_gate, w_up, w_down):
    hp, hs = x_prompt, x_sample
    kp, vp, sp, cp, ksm, vsm, ssm_s, cs = [], [], [], [], [], [], [], []
    q_pos_s = PAST_LEN + jnp.arange(DEC_SEQ)
    k_pos_s = jnp.arange(PAST_LEN + DEC_SEQ)
    for l in range(DEPTH):
        q, k, v, z, xbc, dt_raw = split_proj(hp, w_in[l])
        att = stick_breaking_prompt(q, k, v, att_bias[l])
        buf = jnp.pad(xbc, ((0, 0), (CONV_W - 1, 0), (0, 0)))
        xc = jax.nn.silu(causal_conv(buf, conv_w[l], conv_b[l]))
        xs, dt, a, bm, cm = ssd_inputs(xc, dt_raw, dt_bias[l], a_log[l])
        y, h_fin = ssd_chunked(xs, dt, a, bm, cm)
        mix = merge_heads(att, y, xs, z, d_skip[l], attn_norm_g[l], ssm_norm_g[l], w_out[l])
        hp = layer_norm(DN_ALPHA * hp + mix, ln1_g[l], ln1_b[l])
        hp = layer_norm(DN_ALPHA * hp + moe(hp, w_router, b_router, w_gate[l], w_up[l], w_down[l]), ln2_g[l], ln2_b[l])
        kp.append(k)
        vp.append(v)
        sp.append(h_fin.astype(hp.dtype))
        cp.append(xbc[:, -(CONV_W - 1):])
        q, k, v, z, xbc, dt_raw = split_proj(hs, w_in[l])
        k_past = cache_k[l][page_table].reshape(DEC_BATCH, -1, ATT_HEADS, ATT_HEAD_DIM)
        v_past = cache_v[l][page_table].reshape(DEC_BATCH, -1, ATT_HEADS, ATT_HEAD_DIM)
        k_all = jnp.concatenate([k_past.astype(k.dtype), k], axis=1)
        v_all = jnp.concatenate([v_past.astype(v.dtype), v], axis=1)
        att = stick_breaking(q, k_all, v_all, q_pos_s, k_pos_s, att_bias[l])
        buf = jnp.concatenate([state_conv[l].astype(xbc.dtype), xbc], axis=1)
        xc = jax.nn.silu(causal_conv(buf, conv_w[l], conv_b[l]))
        xs, dt, a, bm, cm = ssd_inputs(xc, dt_raw, dt_bias[l], a_log[l])
        y, h_new = ssd_recurrent(state_ssm[l], xs, dt, a, bm, cm)
        mix = merge_heads(att, y, xs, z, d_skip[l], attn_norm_g[l], ssm_norm_g[l], w_out[l])
        hs = layer_norm(DN_ALPHA * hs + mix, ln1_g[l], ln1_b[l])
        hs = layer_norm(DN_ALPHA * hs + moe(hs, w_router, b_router, w_gate[l], w_up[l], w_down[l]), ln2_g[l], ln2_b[l])
        ksm.append(k)
        vsm.append(v)
        ssm_s.append(h_new.astype(hs.dtype))
        cs.append(buf[:, -(CONV_W - 1):])
    return (hp, hs, jnp.stack(kp), jnp.stack(vp), jnp.stack(sp), jnp.stack(cp),
            jnp.stack(ksm), jnp.stack(vsm), jnp.stack(ssm_s), jnp.stack(cs))
```

```python
import functools
import math

import jax
import jax.numpy as jnp
from jax import lax
from jax.experimental import pallas as pl
from jax.experimental.pallas import tpu as pltpu

F32 = jnp.float32
BF16 = jnp.bfloat16

D_MODEL = 1024
DEPTH = 4
HEADS = 8
HEAD_DIM = 64
ATT_W = HEADS * HEAD_DIM
SSM_W = HEADS * HEAD_DIM
SSM_STATE = 128
SSM_GROUPS = 2
HEADS_PER_GROUP = HEADS // SSM_GROUPS
CONV_W = 4
CONV_DIM = SSM_W + 2 * SSM_GROUPS * SSM_STATE
IN_MAIN = 3 * ATT_W + SSM_W + CONV_DIM
N_EXPERTS = 16
N_EXPERT_GROUPS = 4
EXPERTS_PER_GROUP = 4
N_PAIRS = 6
N_BUCKETS = N_EXPERT_GROUPS * N_PAIRS
D_FF = D_MODEL // 2
DN_ALPHA = (2 * DEPTH) ** 0.25
NORM_EPS = 1e-5
QK_SCALE = HEAD_DIM ** -0.5

LANES = 128
SUBLANES = 8
MXU_DIM = 256
VMEM_LIMIT = 56 * 1024 * 1024

TM_PROJ = 512
TQ = MXU_DIM
TK = MXU_DIM
SSD_CHUNK = 128
TM_MOE = 256
DMA_WAVE = 128


def _cparams(sem):
    return pltpu.CompilerParams(dimension_semantics=sem, vmem_limit_bytes=VMEM_LIMIT)


def _sigmoid(x):
    return 1.0 / (1.0 + jnp.exp(-x))


def _softplus(x):
    return jnp.maximum(x, 0.0) + jnp.log(1.0 + jnp.exp(-jnp.abs(x)))


def _split3(x):
    hi = x.astype(BF16)
    r = x - hi.astype(F32)
    mid = r.astype(BF16)
    lo = (r - mid.astype(F32)).astype(BF16)
    return hi, mid, lo


def _dot(a, b):
    return jnp.dot(a, b, preferred_element_type=F32)


def _dot_nt(a, b):
    return lax.dot_general(a, b, (((1,), (1,)), ((), ())), preferred_element_type=F32)


def _inproj_body(x_ref, w_ref, wdt_ref, q_ref, ktb_ref, vtb_ref, kt_ref, vt_ref, z_ref, xbc_ref, dt_ref):
    x = x_ref[...].astype(BF16)

    def mm(lo, hi):
        return _dot_nt(x, w_ref[lo:hi, :])

    def mm_t(lo, hi):
        return _dot_nt(w_ref[lo:hi, :], x)

    q = mm(0, ATT_W) * QK_SCALE
    for h in range(HEADS):
        q_ref[h] = q[:, h * HEAD_DIM:(h + 1) * HEAD_DIM].astype(BF16)
    kt = mm_t(ATT_W, 2 * ATT_W)
    kt_ref[...] = kt
    ktb_ref[...] = kt.astype(BF16)
    vt = mm_t(2 * ATT_W, 3 * ATT_W)
    vt_ref[...] = vt
    vtb_ref[...] = vt.astype(BF16)
    z_ref[...] = mm(3 * ATT_W, 3 * ATT_W + SSM_W)
    xbc_ref[...] = mm(3 * ATT_W + SSM_W, IN_MAIN)
    dt_ref[...] = _dot_nt(x, wdt_ref[...])


def _inproj(x, w_t, wdt_t, nseq, tm):
    t = x.shape[0]
    seq = t // nseq
    per_seq = seq // tm
    row = lambda i: (i, 0)
    head = lambda i: (0, i, 0)
    tr = lambda i: (i // per_seq, 0, i % per_seq)
    const = lambda i: (0, 0)
    return pl.pallas_call(
        _inproj_body,
        grid=(t // tm,),
        in_specs=[pl.BlockSpec((tm, D_MODEL), row),
                  pl.BlockSpec((IN_MAIN, D_MODEL), const),
                  pl.BlockSpec((LANES, D_MODEL), const)],
        out_specs=[pl.BlockSpec((HEADS, tm, HEAD_DIM), head),
                   pl.BlockSpec((None, ATT_W, tm), tr),
                   pl.BlockSpec((None, ATT_W, tm), tr),
                   pl.BlockSpec((None, ATT_W, tm), tr),
                   pl.BlockSpec((None, ATT_W, tm), tr),
                   pl.BlockSpec((tm, SSM_W), row),
                   pl.BlockSpec((tm, CONV_DIM), row),
                   pl.BlockSpec((tm, LANES), row)],
        out_shape=[jax.ShapeDtypeStruct((HEADS, t, HEAD_DIM), BF16),
                   jax.ShapeDtypeStruct((nseq, ATT_W, seq), BF16),
                   jax.ShapeDtypeStruct((nseq, ATT_W, seq), BF16),
                   jax.ShapeDtypeStruct((nseq, ATT_W, seq), F32),
                   jax.ShapeDtypeStruct((nseq, ATT_W, seq), F32),
                   jax.ShapeDtypeStruct((t, SSM_W), F32),
                   jax.ShapeDtypeStruct((t, CONV_DIM), F32),
                   jax.ShapeDtypeStruct((t, LANES), F32)],
        compiler_params=_cparams(("arbitrary",)),
        name="inproj",
    )(x, w_t, wdt_t)


def _stick_tile(q, kt, vt, bias, u, mask):
    z = _dot(q, kt) + bias
    log_beta = jnp.minimum(z, 0.0) - jnp.log(1.0 + jnp.exp(-jnp.abs(z)))
    l1m = log_beta - z
    if mask is not None:
        l1m = jnp.where(mask, l1m, 0.0)
    suffix = _dot(l1m.astype(BF16), u)
    p = jnp.exp(log_beta + suffix)
    if mask is not None:
        p = jnp.where(mask, p, 0.0)
    pv = _dot_nt(p.astype(BF16), vt)
    return pv, jnp.sum(l1m, axis=1, keepdims=True)


def _attn_body(bias_ref, q_ref, k_ref, v_ref, u_ref, o_ref, acc_ref, c_ref):
    qi = pl.program_id(1)
    h = pl.program_id(2)
    q = q_ref[0]
    bias = bias_ref[h]
    u = u_ref[...]

    row = lax.broadcasted_iota(jnp.int32, (TQ, TK), 0)
    col = lax.broadcasted_iota(jnp.int32, (TQ, TK), 1)
    start = pl.multiple_of(qi * TK, TK)
    pv, dc = _stick_tile(q, k_ref[:, pl.ds(start, TK)], v_ref[:, pl.ds(start, TK)],
                         bias, u, col < row)
    acc_ref[...] = pv
    c_ref[...] = dc

    def step(jj, carry):
        s0 = pl.multiple_of((qi - 1 - jj) * TK, TK)
        pv, dc = _stick_tile(q, k_ref[:, pl.ds(s0, TK)], v_ref[:, pl.ds(s0, TK)],
                             bias, u, None)
        c = c_ref[...]
        acc_ref[...] += jnp.exp(c) * pv
        c_ref[...] = c + dc
        return carry

    lax.fori_loop(0, qi, step, 0)

    for hh in range(HEADS):
        @pl.when(h == hh)
        def _():
            o_ref[:, hh * HEAD_DIM:(hh + 1) * HEAD_DIM] = acc_ref[...]


def _suffix_matrix(n):
    j = jnp.arange(n)
    return (j[:, None] > j[None, :]).astype(BF16)


def _attn_prompt(q8, kb, vb, bias, bsz, seq):
    nq = seq // TQ
    return pl.pallas_call(
        _attn_body,
        grid_spec=pltpu.PrefetchScalarGridSpec(
            num_scalar_prefetch=1,
            grid=(bsz, nq, HEADS),
            in_specs=[pl.BlockSpec((1, TQ, HEAD_DIM), lambda b, i, h, bias: (h, b * nq + i, 0)),
                      pl.BlockSpec((None, HEAD_DIM, seq), lambda b, i, h, bias: (b, h, 0)),
                      pl.BlockSpec((None, HEAD_DIM, seq), lambda b, i, h, bias: (b, h, 0)),
                      pl.BlockSpec((TK, TK), lambda b, i, h, bias: (0, 0))],
            out_specs=pl.BlockSpec((TQ, ATT_W), lambda b, i, h, bias: (b * nq + i, 0)),
            scratch_shapes=[pltpu.VMEM((TQ, HEAD_DIM), F32), pltpu.VMEM((TQ, 1), F32)]),
        out_shape=jax.ShapeDtypeStruct((bsz * seq, ATT_W), F32),
        compiler_params=_cparams(("arbitrary", "arbitrary", "arbitrary")),
        name="attn_prompt",
    )(bias, q8, kb, vb, _suffix_matrix(TK))


def _attn_decode_body(n_pages, layer, pt_ref, q_ref, bias_ref, u_ref, hm_ref, k_hbm, v_hbm,
                      o_ref, kbuf, vbuf, sem):
    b = pl.program_id(0)
    slot = b % 2

    def page_copies(seq_idx, s):
        copies = []
        for p in range(n_pages):
            pg = pt_ref[seq_idx, p]
            copies.append(pltpu.make_async_copy(k_hbm.at[layer, pg], kbuf.at[s, p], sem.at[0, s]))
            copies.append(pltpu.make_async_copy(v_hbm.at[layer, pg], vbuf.at[s, p], sem.at[1, s]))
        return copies

    @pl.when(b == 0)
    def _():
        for cp in page_copies(0, 0):
            cp.start()

    @pl.when(b + 1 < pl.num_programs(0))
    def _():
        for cp in page_copies(b + 1, 1 - slot):
            cp.start()

    for cp in page_copies(b, slot):
        cp.wait()

    q = q_ref[0]
    bias = bias_ref[...]
    u = u_ref[...]
    c = jnp.zeros((2 * HEADS, 1), F32)
    acc = jnp.zeros((2 * HEADS, ATT_W), F32)
    for p in reversed(range(n_pages)):
        kp = kbuf[slot, p].astype(BF16)
        vp = vbuf[slot, p].astype(BF16)
        z = _dot(q, kp) + bias
        log_beta = jnp.minimum(z, 0.0) - jnp.log(1.0 + jnp.exp(-jnp.abs(z)))
        l1m = log_beta - z
        hi, mid, lo = _split3(l1m)
        suffix = _dot(hi, u) + _dot(mid, u) + _dot(lo, u)
        w = jnp.exp(log_beta + suffix + c)
        acc = acc + _dot_nt(w.astype(BF16), vp)
        c = c + jnp.sum(l1m, axis=1, keepdims=True)
    o_ref[0] = jnp.sum(acc * hm_ref[...], axis=0, keepdims=True)


def _attn_decode(layer, q_bd, bias_b, cache_k4, cache_v4, page_table):
    nb, n_pages = page_table.shape
    page = cache_k4.shape[3]
    hm = (jnp.arange(ATT_W)[None, :] // HEAD_DIM == jnp.arange(2 * HEADS)[:, None]).astype(F32)
    const = lambda b, pt: (0, 0)
    any_spec = pl.BlockSpec(memory_space=pl.ANY)
    return pl.pallas_call(
        functools.partial(_attn_decode_body, n_pages, layer),
        grid_spec=pltpu.PrefetchScalarGridSpec(
            num_scalar_prefetch=1,
            grid=(nb,),
            in_specs=[pl.BlockSpec((1, 2 * HEADS, ATT_W), lambda b, pt: (b, 0, 0)),
                      pl.BlockSpec((2 * HEADS, page), const),
                      pl.BlockSpec((page, page), const),
                      pl.BlockSpec((2 * HEADS, ATT_W), const),
                      any_spec, any_spec],
            out_specs=pl.BlockSpec((1, 1, ATT_W), lambda b, pt: (b, 0, 0)),
            scratch_shapes=[pltpu.VMEM((2, n_pages, ATT_W, page), F32),
                            pltpu.VMEM((2, n_pages, ATT_W, page), F32),
                            pltpu.SemaphoreType.DMA((2, 2))]),
        out_shape=jax.ShapeDtypeStruct((nb, 1, ATT_W), F32),
        compiler_params=_cparams(("arbitrary",)),
        name="attn_decode",
    )(page_table, q_bd, bias_b, _suffix_matrix(page), hm, cache_k4, cache_v4)


def _ssd_body(xbc_ref, dt_ref, z_ref, cw_ref, cb_ref, dtb_ref, alog_ref, dskip_ref, g_ref, tri_ref,
              y_ref, state_ref, tail_ref, ext_ref, s_ref):
    c = pl.program_id(1)
    nc = pl.num_programs(1)
    q = SSD_CHUNK
    pad = SUBLANES

    @pl.when(c == 0)
    def _():
        ext_ref[0:pad, :] = jnp.zeros((pad, CONV_DIM), F32)
        s_ref[...] = jnp.zeros_like(s_ref)

    @pl.when(c > 0)
    def _():
        ext_ref[0:pad, :] = ext_ref[q:q + pad, :]

    raw = xbc_ref[...]
    ext_ref[pad:pad + q, :] = raw

    @pl.when(c == nc - 1)
    def _():
        tail_ref[0] = raw[q - (CONV_W - 1):, :]

    conv = raw * cw_ref[CONV_W - 1:CONV_W, :] + cb_ref[...]
    for i in range(1, CONV_W):
        conv = conv + ext_ref[pad - i:pad - i + q, :] * cw_ref[CONV_W - 1 - i:CONV_W - i, :]
    xc = conv * _sigmoid(conv)
    xs = xc[:, :SSM_W]
    bm = xc[:, SSM_W:SSM_W + SSM_GROUPS * SSM_STATE]
    cm = xc[:, SSM_W + SSM_GROUPS * SSM_STATE:]

    dt = _softplus(dt_ref[...] + dtb_ref[...])
    a = -jnp.exp(alog_ref[...])
    da = dt * a
    tri = tri_ref[...]
    hi, mid, lo = _split3(da)
    acum = _dot(tri, hi) + _dot(tri, mid) + _dot(tri, lo)
    acum_t = acum.T
    row = lax.broadcasted_iota(jnp.int32, (q, q), 0)
    col = lax.broadcasted_iota(jnp.int32, (q, q), 1)
    causal = col <= row

    y_parts = []
    for g in range(SSM_GROUPS):
        bm_g = bm[:, g * SSM_STATE:(g + 1) * SSM_STATE]
        cm_g = cm[:, g * SSM_STATE:(g + 1) * SSM_STATE].astype(BF16)
        scores = _dot_nt(cm_g, bm_g.astype(BF16))
        gsl = slice(g * HEADS_PER_GROUP * HEAD_DIM, (g + 1) * HEADS_PER_GROUP * HEAD_DIM)
        s_prev = s_ref[gsl, :]
        y_off = _dot_nt(cm_g, s_prev.astype(BF16))
        xdt_parts = []
        for hl in range(HEADS_PER_GROUP):
            h = g * HEADS_PER_GROUP + hl
            hs = slice(h * HEAD_DIM, (h + 1) * HEAD_DIM)
            a_col = acum[:, h:h + 1]
            a_row = acum_t[h:h + 1, :]
            a_last = acum_t[h:h + 1, q - 1:q]
            decay = jnp.where(causal, jnp.exp(a_col - a_row), 0.0)
            xdt = xs[:, hs] * dt[:, h:h + 1]
            y_h = _dot((scores * decay).astype(BF16), xdt.astype(BF16))
            y_h = y_h + y_off[:, hl * HEAD_DIM:(hl + 1) * HEAD_DIM] * jnp.exp(a_col)
            y_parts.append(y_h)
            xdt_parts.append(xdt * jnp.exp(a_last - a_col))
            s_ref[hs, :] = s_ref[hs, :] * jnp.exp(a_last)
        xdt_g = jnp.concatenate(xdt_parts, axis=1)
        s_ref[gsl, :] += _dot(xdt_g.T.astype(BF16), bm_g.astype(BF16))
    y = jnp.concatenate(y_parts, axis=1)

    y = y + xs * dskip_ref[...]
    y = y * (z_ref[...] * _sigmoid(z_ref[...]))
    y = y * lax.rsqrt(jnp.mean(y * y, axis=1, keepdims=True) + NORM_EPS) * g_ref[...]
    y_ref[...] = y.astype(BF16)

    @pl.when(c == nc - 1)
    def _():
        state_ref[0] = s_ref[...].reshape(HEADS, HEAD_DIM, SSM_STATE)


def _row128(v):
    return jnp.pad(v.astype(F32), (0, LANES - v.shape[0]))[None, :]


def _ssd_prompt(xbc, dt_raw, z, conv_w, conv_b, dt_bias, a_log, d_skip, g_ssm, bsz, seq):
    q = SSD_CHUNK
    nc = seq // q
    j = jnp.arange(q)
    tri = (j[None, :] <= j[:, None]).astype(BF16)
    row = lambda b, c: (b * nc + c, 0)
    const = lambda b, c: (0, 0)
    return pl.pallas_call(
        _ssd_body,
        grid=(bsz, nc),
        in_specs=[pl.BlockSpec((q, CONV_DIM), row),
                  pl.BlockSpec((q, LANES), row),
                  pl.BlockSpec((q, SSM_W), row),
                  pl.BlockSpec((CONV_W, CONV_DIM), const),
                  pl.BlockSpec((1, CONV_DIM), const),
                  pl.BlockSpec((1, LANES), const),
                  pl.BlockSpec((1, LANES), const),
                  pl.BlockSpec((1, SSM_W), const),
                  pl.BlockSpec((1, SSM_W), const),
                  pl.BlockSpec((q, q), const)],
        out_specs=[pl.BlockSpec((q, SSM_W), row),
                   pl.BlockSpec((1, HEADS, HEAD_DIM, SSM_STATE), lambda b, c: (b, 0, 0, 0)),
                   pl.BlockSpec((1, CONV_W - 1, CONV_DIM), lambda b, c: (b, 0, 0))],
        out_shape=[jax.ShapeDtypeStruct((bsz * seq, SSM_W), BF16),
                   jax.ShapeDtypeStruct((bsz, HEADS, HEAD_DIM, SSM_STATE), F32),
                   jax.ShapeDtypeStruct((bsz, CONV_W - 1, CONV_DIM), F32)],
        scratch_shapes=[pltpu.VMEM((q + SUBLANES, CONV_DIM), F32),
                        pltpu.VMEM((SSM_W, SSM_STATE), F32)],
        compiler_params=_cparams(("arbitrary", "arbitrary")),
        name="ssd_prompt",
    )(xbc, dt_raw, z, conv_w, conv_b[None, :], _row128(dt_bias), _row128(a_log),
      jnp.repeat(d_skip.astype(F32), HEAD_DIM)[None, :], g_ssm[None, :], tri)


def _ssd_dec_pre_body(xbc_ref, sc_ref, dt_ref, cw_ref, cb_ref, dtb_ref, alog_ref,
                      xs_ref, xdt_ref, bm_ref, cm_ref, dec_ref, cnew_ref):
    raw = xbc_ref[...]
    conv = raw * cw_ref[CONV_W - 1:CONV_W, :] + cb_ref[...]
    for i in range(CONV_W - 1):
        conv = conv + sc_ref[i] * cw_ref[i:i + 1, :]
    for i in range(CONV_W - 2):
        cnew_ref[i] = sc_ref[i + 1]
    cnew_ref[CONV_W - 2] = raw
    xc = conv * _sigmoid(conv)
    xs = xc[:, :SSM_W]
    xs_ref[...] = xs
    bm_ref[...] = xc[:, SSM_W:SSM_W + SSM_GROUPS * SSM_STATE]
    cm_ref[...] = xc[:, SSM_W + SSM_GROUPS * SSM_STATE:]
    dt = _softplus(dt_ref[...] + dtb_ref[...])
    dec_ref[...] = jnp.exp(dt * (-jnp.exp(alog_ref[...])))
    for h in range(HEADS):
        hs = slice(h * HEAD_DIM, (h + 1) * HEAD_DIM)
        xdt_ref[:, hs] = xs[:, hs] * dt[:, h:h + 1]


def _ssd_dec_pre(xbc, state_conv_l, dt_raw, conv_w, conv_b, dt_bias, a_log):
    nb = xbc.shape[0]
    gn = SSM_GROUPS * SSM_STATE
    return pl.pallas_call(
        _ssd_dec_pre_body,
        out_shape=[jax.ShapeDtypeStruct((nb, SSM_W), F32),
                   jax.ShapeDtypeStruct((nb, SSM_W), F32),
                   jax.ShapeDtypeStruct((nb, gn), F32),
                   jax.ShapeDtypeStruct((nb, gn), F32),
                   jax.ShapeDtypeStruct((nb, LANES), F32),
                   jax.ShapeDtypeStruct((CONV_W - 1, nb, CONV_DIM), F32)],
        compiler_params=pltpu.CompilerParams(vmem_limit_bytes=VMEM_LIMIT),
        name="ssd_decode_pre",
    )(xbc, state_conv_l, dt_raw, conv_w, conv_b[None, :], _row128(dt_bias), _row128(a_log))


SEQ_PER_STEP = 8


def _ssd_dec_state_body(dec_ref, xdt_ref, bm_ref, cm_ref, s_ref, snew_ref, y_ref):
    step = pl.program_id(0)
    for i in range(SEQ_PER_STEP):
        b = step * SEQ_PER_STEP + i
        for h in range(HEADS):
            g = h // HEADS_PER_GROUP
            hs = slice(h * HEAD_DIM, (h + 1) * HEAD_DIM)
            ns = slice(g * SSM_STATE, (g + 1) * SSM_STATE)
            xcol = xdt_ref[0, hs, i:i + 1]
            brow = bm_ref[0, i:i + 1, ns]
            crow = cm_ref[0, i:i + 1, ns]
            hn = s_ref[i, h] * dec_ref[b, h] + xcol * brow
            snew_ref[i, h] = hn
            y_ref[0, hs, i:i + 1] = jnp.sum(hn * crow, axis=1, keepdims=True)


def _ssd_dec_state(dec, xdt_t, bm3, cm3, state_l):
    nb = state_l.shape[0]
    steps = nb // SEQ_PER_STEP
    gn = SSM_GROUPS * SSM_STATE
    blk3 = lambda s: (s, 0, 0)
    blk4 = lambda s: (s, 0, 0, 0)
    return pl.pallas_call(
        _ssd_dec_state_body,
        grid=(steps,),
        in_specs=[pl.BlockSpec(memory_space=pltpu.SMEM),
                  pl.BlockSpec((1, SSM_W, SEQ_PER_STEP), blk3),
                  pl.BlockSpec((1, SEQ_PER_STEP, gn), blk3),
                  pl.BlockSpec((1, SEQ_PER_STEP, gn), blk3),
                  pl.BlockSpec((SEQ_PER_STEP, HEADS, HEAD_DIM, SSM_STATE), blk4)],
        out_specs=[pl.BlockSpec((SEQ_PER_STEP, HEADS, HEAD_DIM, SSM_STATE), blk4),
                   pl.BlockSpec((1, SSM_W, SEQ_PER_STEP), blk3)],
        out_shape=[jax.ShapeDtypeStruct(state_l.shape, F32),
                   jax.ShapeDtypeStruct((steps, SSM_W, SEQ_PER_STEP), F32)],
        compiler_params=_cparams(("arbitrary",)),
        name="ssd_decode_state",
    )(dec, xdt_t, bm3, cm3, state_l)


def _gate_norm_body(y_ref, xs_ref, z_ref, dskip_ref, g_ref, o_ref):
    y = y_ref[...] + xs_ref[...] * dskip_ref[...]
    z = z_ref[...]
    y = y * (z * _sigmoid(z))
    y = y * lax.rsqrt(jnp.mean(y * y, axis=1, keepdims=True) + NORM_EPS) * g_ref[...]
    o_ref[...] = y.astype(BF16)


def _gate_norm(y, xs, z, d_skip, g_ssm):
    return pl.pallas_call(
        _gate_norm_body,
        out_shape=jax.ShapeDtypeStruct(y.shape, BF16),
        name="ssd_decode_gate",
    )(y, xs, z, jnp.repeat(d_skip.astype(F32), HEAD_DIM)[None, :], g_ssm[None, :])


def _layer_norm(u, g, b):
    mu = jnp.mean(u, axis=1, keepdims=True)
    d = u - mu
    var = jnp.mean(d * d, axis=1, keepdims=True)
    return d * lax.rsqrt(var + NORM_EPS) * g + b


def _outproj_body(att_ref, yg_ref, h_ref, wo_ref, gatt_ref, g1_ref, b1_ref, wr_ref, br_ref,
                  h1_ref, lg_ref):
    att = att_ref[...]
    att = att * lax.rsqrt(jnp.mean(att * att, axis=1, keepdims=True) + NORM_EPS) * gatt_ref[...]
    mix = _dot(att.astype(BF16), wo_ref[0:ATT_W, :]) + _dot(yg_ref[...], wo_ref[ATT_W:, :])
    h1 = _layer_norm(DN_ALPHA * h_ref[...] + mix, g1_ref[...], b1_ref[...])
    h1_ref[...] = h1
    lg_ref[...] = _dot_nt(h1.astype(BF16), wr_ref[...]) + br_ref[...]


def _outproj(att, yg, h, w_out_b, g_att, g1, b1, w_router_p, b_router_p, tm):
    t = att.shape[0]
    row = lambda i: (i, 0)
    const = lambda i: (0, 0)
    in_specs = [pl.BlockSpec((tm, ATT_W), row),
                pl.BlockSpec((tm, SSM_W), row),
                pl.BlockSpec((tm, D_MODEL), row),
                pl.BlockSpec((D_MODEL, D_MODEL), const),
                pl.BlockSpec((1, ATT_W), const),
                pl.BlockSpec((1, D_MODEL), const),
                pl.BlockSpec((1, D_MODEL), const),
                pl.BlockSpec((LANES, D_MODEL), const),
                pl.BlockSpec((1, LANES), const)]
    args = [att, yg, h, w_out_b, g_att[None, :], g1[None, :], b1[None, :], w_router_p, b_router_p]
    return pl.pallas_call(
        _outproj_body,
        grid=(t // tm,),
        in_specs=in_specs,
        out_specs=[pl.BlockSpec((tm, D_MODEL), row), pl.BlockSpec((tm, LANES), row)],
        out_shape=[jax.ShapeDtypeStruct((t, D_MODEL), F32),
                   jax.ShapeDtypeStruct((t, LANES), F32)],
        compiler_params=_cparams(("arbitrary",)),
        name="outproj",
    )(*args)


def _row_permute_body(n_rows, gather, idx_ref, *refs):
    src_ref, dst_ref, sem = refs[0], refs[-2], refs[-1]
    n_waves = n_rows // DMA_WAVE

    def copy(t):
        j = idx_ref[t]
        if gather:
            return pltpu.make_async_copy(src_ref.at[pl.ds(j, 1)], dst_ref.at[pl.ds(t, 1)], sem)
        return pltpu.make_async_copy(src_ref.at[pl.ds(t, 1)], dst_ref.at[pl.ds(j, 1)], sem)

    def issue(w):
        def one(r, carry):
            copy(w * DMA_WAVE + r).start()
            return carry
        lax.fori_loop(0, DMA_WAVE, one, 0, unroll=8)

    def drain(w):
        def one(r, carry):
            copy(w * DMA_WAVE + r).wait()
            return carry
        lax.fori_loop(0, DMA_WAVE, one, 0, unroll=8)

    issue(0)

    def wave(w, carry):
        issue(w)
        drain(w - 1)
        return carry

    lax.fori_loop(1, n_waves, wave, 0)
    drain(n_waves - 1)


def _row_permute(idx, src, dst, gather):
    n_rows = idx.shape[0]
    any_spec = pl.BlockSpec(memory_space=pl.ANY)
    if gather:
        args, in_specs, aliases = (idx, src), [any_spec], {}
        out_shape = jax.ShapeDtypeStruct((n_rows, src.shape[1]), src.dtype)
    else:
        args, in_specs, aliases = (idx, src, dst), [any_spec, any_spec], {2: 0}
        out_shape = jax.ShapeDtypeStruct(dst.shape, dst.dtype)
    return pl.pallas_call(
        functools.partial(_row_permute_body, n_rows, gather),
        grid_spec=pltpu.PrefetchScalarGridSpec(
            num_scalar_prefetch=1,
            grid=(1,),
            in_specs=in_specs,
            out_specs=any_spec,
            scratch_shapes=[pltpu.SemaphoreType.DMA]),
        out_shape=out_shape,
        input_output_aliases=aliases,
        compiler_params=_cparams(("arbitrary",)),
        name="row_gather" if gather else "row_scatter",
    )(*args)


def _moe_body(lo_ref, hi_ref, first_ref, nact_ref,
              x_ref, wr_ref, br_ref, g2_ref, b2_ref,
              wg_lo, wu_lo, wd_lo, wg_hi, wu_hi, wd_hi,
              o_ref, wgb, wub, wdb):
    i = pl.program_id(0)

    @pl.when(i < nact_ref[0])
    def _():
        @pl.when(first_ref[i] == 1)
        def _():
            wgb[0] = wg_lo[...].astype(BF16)
            wub[0] = wu_lo[...].astype(BF16)
            wdb[0] = wd_lo[...].astype(BF16)
            wgb[1] = wg_hi[...].astype(BF16)
            wub[1] = wu_hi[...].astype(BF16)
            wdb[1] = wd_hi[...].astype(BF16)

        x = x_ref[...]
        xb = x.astype(BF16)
        logits = _dot_nt(xb, wr_ref[...]) + br_ref[...]
        lane = lax.broadcasted_iota(jnp.int32, logits.shape, 1)
        valid = lane < N_EXPERTS
        m = jnp.max(jnp.where(valid, logits, -jnp.inf), axis=1, keepdims=True)
        e = jnp.where(valid, jnp.exp(logits - m), 0.0)
        probs = e / jnp.sum(e, axis=1, keepdims=True)
        p_lo = jnp.sum(jnp.where(lane == lo_ref[i], probs, 0.0), axis=1, keepdims=True)
        p_hi = jnp.sum(jnp.where(lane == hi_ref[i], probs, 0.0), axis=1, keepdims=True)
        denom = p_lo + p_hi
        out = jnp.zeros_like(x)
        for slot, gate in ((0, p_lo / denom), (1, p_hi / denom)):
            hg = _dot(xb, wgb[slot])
            hu = _dot(xb, wub[slot])
            hmid = (hg * _sigmoid(hg)) * hu
            out = out + gate * _dot(hmid.astype(BF16), wdb[slot])
        o_ref[...] = _layer_norm(DN_ALPHA * x + out, g2_ref[...], b2_ref[...])

    @pl.when(i >= nact_ref[0])
    def _():
        o_ref[...] = jnp.zeros_like(o_ref)


def _moe_sorted(layer, x_sorted, tile_lo, tile_hi, tile_first, n_active,
                w_router_p, b_router_p, g2, b2, w_gate, w_up, w_down):
    n_tiles = tile_lo.shape[0]
    tm = TM_MOE

    def xmap(i, lo, hi, first, nact):
        return (jnp.minimum(i, nact[0] - 1), 0)

    const = lambda i, lo, hi, first, nact: (0, 0)
    w_lo = lambda i, lo, hi, first, nact: (layer, lo[i], 0, 0)
    w_hi = lambda i, lo, hi, first, nact: (layer, hi[i], 0, 0)
    up_blk = (None, None, D_MODEL, D_FF)
    dn_blk = (None, None, D_FF, D_MODEL)
    return pl.pallas_call(
        _moe_body,
        grid_spec=pltpu.PrefetchScalarGridSpec(
            num_scalar_prefetch=4,
            grid=(n_tiles,),
            in_specs=[pl.BlockSpec((tm, D_MODEL), xmap),
                      pl.BlockSpec((LANES, D_MODEL), const),
                      pl.BlockSpec((1, LANES), const),
                      pl.BlockSpec((1, D_MODEL), const),
                      pl.BlockSpec((1, D_MODEL), const),
                      pl.BlockSpec(up_blk, w_lo), pl.BlockSpec(up_blk, w_lo), pl.BlockSpec(dn_blk, w_lo),
                      pl.BlockSpec(up_blk, w_hi), pl.BlockSpec(up_blk, w_hi), pl.BlockSpec(dn_blk, w_hi)],
            out_specs=pl.BlockSpec((tm, D_MODEL), lambda i, lo, hi, first, nact: (i, 0)),
            scratch_shapes=[pltpu.VMEM((2, D_MODEL, D_FF), BF16),
                            pltpu.VMEM((2, D_MODEL, D_FF), BF16),
                            pltpu.VMEM((2, D_FF, D_MODEL), BF16)]),
        out_shape=jax.ShapeDtypeStruct(x_sorted.shape, F32),
        compiler_params=_cparams(("arbitrary",)),
        name="moe",
    )(tile_lo, tile_hi, tile_first, n_active,
      x_sorted, w_router_p, b_router_p, g2[None, :], b2[None, :],
      w_gate, w_up, w_down, w_gate, w_up, w_down)


def _route(logits, n_tiles):
    t = logits.shape[0]
    probs = jax.nn.softmax(logits, axis=-1)
    grouped = probs.reshape(t, N_EXPERT_GROUPS, EXPERTS_PER_GROUP)
    g_sel = jnp.argmax(grouped.max(-1), axis=-1)
    in_group = jnp.take_along_axis(grouped, g_sel[:, None, None], axis=1)[:, 0, :]
    _, top_i = lax.top_k(in_group, 2)
    e_lo = jnp.min(top_i, axis=-1)
    e_hi = jnp.max(top_i, axis=-1)
    pair = e_lo * (2 * EXPERTS_PER_GROUP - 1 - e_lo) // 2 + (e_hi - e_lo - 1)
    bucket = (g_sel * N_PAIRS + pair).astype(jnp.int32)

    onehot = (bucket[:, None] == jnp.arange(N_BUCKETS, dtype=jnp.int32)[None, :]).astype(jnp.int32)
    rank = jnp.sum((jnp.cumsum(onehot, axis=0) - onehot) * onehot, axis=1)
    counts = jnp.sum(onehot, axis=0)
    tiles_b = (counts + TM_MOE - 1) // TM_MOE
    tile_end = jnp.cumsum(tiles_b)
    tile_start = tile_end - tiles_b
    pos = (tile_start[bucket] * TM_MOE + rank).astype(jnp.int32)
    n_active = tile_end[-1]

    tile_id = jnp.minimum(jnp.arange(n_tiles, dtype=jnp.int32), n_active - 1)
    tile_bucket = jnp.sum((tile_id[:, None] >= tile_end[None, :]).astype(jnp.int32), axis=1)
    grp = tile_bucket // N_PAIRS
    pr = tile_bucket % N_PAIRS
    pair_lo = jnp.array([0, 0, 0, 1, 1, 2], jnp.int32)
    pair_hi = jnp.array([1, 2, 3, 2, 3, 3], jnp.int32)
    tile_lo = (grp * EXPERTS_PER_GROUP + pair_lo[pr]).astype(jnp.int32)
    tile_hi = (grp * EXPERTS_PER_GROUP + pair_hi[pr]).astype(jnp.int32)
    prev = jnp.concatenate([jnp.full((1,), -1, jnp.int32), tile_bucket[:-1].astype(jnp.int32)])
    tile_first = (tile_bucket != prev).astype(jnp.int32)
    return pos, tile_lo, tile_hi, tile_first, n_active.astype(jnp.int32)[None]


def kernel(x_prompt, x_sample, cache_k, cache_v, state_ssm, state_conv, page_table, w_in, att_bias,
           conv_w, conv_b, dt_bias, a_log, d_skip, attn_norm_g, ssm_norm_g, w_out, ln1_g, ln1_b,
           ln2_g, ln2_b, w_router, b_router, w_gate, w_up, w_down):
    bsz, seq, _ = x_prompt.shape
    nb = x_sample.shape[0]
    tp = bsz * seq
    t_all = tp + nb
    n_tiles = t_all // TM_MOE + N_BUCKETS
    pool, page = cache_k.shape[1], cache_k.shape[2]

    hp = x_prompt.reshape(tp, D_MODEL)
    hs = x_sample.reshape(nb, D_MODEL)
    cache_k4 = cache_k.transpose(0, 1, 3, 4, 2).reshape(DEPTH, pool, ATT_W, page)
    cache_v4 = cache_v.transpose(0, 1, 3, 4, 2).reshape(DEPTH, pool, ATT_W, page)
    state_conv_t = state_conv.transpose(0, 2, 1, 3)
    w_in_t = w_in.transpose(0, 2, 1).astype(BF16)
    w_main_t = w_in_t[:, :IN_MAIN, :]
    w_dt_t = jnp.pad(w_in_t[:, IN_MAIN:, :], ((0, 0), (0, LANES - HEADS), (0, 0)))
    w_out_b = w_out.astype(BF16)
    w_router_p = jnp.pad(w_router.T.astype(BF16), ((0, LANES - N_EXPERTS), (0, 0)))
    b_router_p = jnp.pad(b_router.astype(F32), (0, LANES - N_EXPERTS))[None, :]
    head_of_row = jnp.arange(2 * HEADS)[:, None]
    head_of_lane = jnp.arange(ATT_W)[None, :] // HEAD_DIM

    outs = [[] for _ in range(8)]
    for l in range(DEPTH):
        q8, kb, vb, k_p, v_p, z_p, xbc_p, dt_p = _inproj(hp, w_main_t[l], w_dt_t[l], bsz, TM_PROJ)
        att_p = _attn_prompt(q8, kb, vb, att_bias[l].astype(F32), bsz, seq)
        yg_p, ssm_p, conv_p = _ssd_prompt(xbc_p, dt_p, z_p, conv_w[l], conv_b[l], dt_bias[l], a_log[l],
                                          d_skip[l], ssm_norm_g[l], bsz, seq)
        q8s, _, _, k_s, v_s, z_s, xbc_s, dt_s = _inproj(hs, w_main_t[l], w_dt_t[l], 1, nb)
        q_rows = q8s.transpose(1, 0, 2).reshape(nb, 1, ATT_W)
        q_bd = jnp.where(head_of_row[None] == head_of_lane[None], q_rows, 0).astype(BF16)
        bias_b = jnp.broadcast_to(
            jnp.pad(att_bias[l].astype(F32), (0, HEADS))[:, None], (2 * HEADS, page))
        att_s = _attn_decode(l, q_bd, bias_b, cache_k4, cache_v4, page_table).reshape(nb, ATT_W)
        xs_s, xdt_s, bm_s, cm_s, dec_s, conv_s = _ssd_dec_pre(
            xbc_s, state_conv_t[l], dt_s, conv_w[l], conv_b[l], dt_bias[l], a_log[l])
        steps = nb // SEQ_PER_STEP
        xdt_t = xdt_s.reshape(steps, SEQ_PER_STEP, SSM_W).transpose(0, 2, 1)
        ssm_s, y_t = _ssd_dec_state(dec_s, xdt_t,
                                    bm_s.reshape(steps, SEQ_PER_STEP, -1),
                                    cm_s.reshape(steps, SEQ_PER_STEP, -1), state_ssm[l])
        y_s = y_t.transpose(0, 2, 1).reshape(nb, SSM_W)
        yg_s = _gate_norm(y_s, xs_s, z_s, d_skip[l], ssm_norm_g[l])
        h1_p, lg_p = _outproj(att_p, yg_p, hp, w_out_b[l], attn_norm_g[l], ln1_g[l], ln1_b[l],
                              w_router_p, b_router_p, TM_PROJ)
        h1_s, lg_s = _outproj(att_s, yg_s, hs, w_out_b[l], attn_norm_g[l], ln1_g[l], ln1_b[l],
                              w_router_p, b_router_p, nb)
        logits = jnp.concatenate([lg_p[:, :N_EXPERTS], lg_s[:, :N_EXPERTS]], axis=0)
        pos, tile_lo, tile_hi, tile_first, n_active = _route(logits, n_tiles)
        x_sorted = jnp.zeros((n_tiles * TM_MOE, D_MODEL), F32)
        x_sorted = _row_permute(pos[:tp], h1_p, x_sorted, gather=False)
        x_sorted = _row_permute(pos[tp:], h1_s, x_sorted, gather=False)
        y_sorted = _moe_sorted(l, x_sorted, tile_lo, tile_hi, tile_first, n_active,
                               w_router_p, b_router_p, ln2_g[l], ln2_b[l], w_gate, w_up, w_down)
        hp = _row_permute(pos[:tp], y_sorted, None, gather=True)
        hs = _row_permute(pos[tp:], y_sorted, None, gather=True)

        for lst, val in zip(outs, (k_p, v_p, ssm_p, conv_p, k_s, v_s, ssm_s, conv_s)):
            lst.append(val)

    k_p, v_p, ssm_p, conv_p, k_s, v_s, ssm_s, conv_s = [jnp.stack(o) for o in outs]

    def untranspose(kt, n, length):
        return kt.reshape(DEPTH, n, HEADS, HEAD_DIM, length).transpose(0, 1, 4, 2, 3)

    return (hp.reshape(bsz, seq, D_MODEL),
            hs.reshape(nb, 1, D_MODEL),
            untranspose(k_p, bsz, seq),
            untranspose(v_p, bsz, seq),
            ssm_p,
            conv_p,
            untranspose(k_s, 1, nb).transpose(0, 2, 1, 3, 4),
            untranspose(v_s, 1, nb).transpose(0, 2, 1, 3, 4),
            ssm_s,
            conv_s.transpose(0, 2, 1, 3))
```

```python
import functools
import math

import jax
import jax.numpy as jnp
from jax import lax
from jax.experimental import pallas as pl
from jax.experimental.pallas import tpu as pltpu

F32 = jnp.float32
BF16 = jnp.bfloat16

D_MODEL = 1024
DEPTH = 4
HEADS = 8
HEAD_DIM = 64
ATT_W = HEADS * HEAD_DIM
SSM_W = HEADS * HEAD_DIM
SSM_STATE = 128
SSM_GROUPS = 2
HEADS_PER_GROUP = HEADS // SSM_GROUPS
CONV_W = 4
CONV_DIM = SSM_W + 2 * SSM_GROUPS * SSM_STATE
IN_MAIN = 3 * ATT_W + SSM_W + CONV_DIM
N_EXPERTS = 16
N_EXPERT_GROUPS = 4
EXPERTS_PER_GROUP = 4
N_PAIRS = 6
N_BUCKETS = N_EXPERT_GROUPS * N_PAIRS
D_FF = D_MODEL // 2
DN_ALPHA = (2 * DEPTH) ** 0.25
NORM_EPS = 1e-5
QK_SCALE = HEAD_DIM ** -0.5

LANES = 128
SUBLANES = 8
MXU_DIM = 256
VMEM_LIMIT = 56 * 1024 * 1024

TM_PROJ = 512
TQ = MXU_DIM
TK = MXU_DIM
SSD_CHUNK = 128
TM_MOE = 256
TM_PERM = 512


def _cparams(sem):
    return pltpu.CompilerParams(dimension_semantics=sem, vmem_limit_bytes=VMEM_LIMIT)


def _sigmoid(x):
    return 1.0 / (1.0 + jnp.exp(-x))


def _softplus(x):
    return jnp.maximum(x, 0.0) + jnp.log(1.0 + jnp.exp(-jnp.abs(x)))


def _split3(x):
    hi = x.astype(BF16)
    r = x - hi.astype(F32)
    mid = r.astype(BF16)
    lo = (r - mid.astype(F32)).astype(BF16)
    return hi, mid, lo


def _dot(a, b):
    return jnp.dot(a, b, preferred_element_type=F32)


def _dot_nt(a, b):
    return lax.dot_general(a, b, (((1,), (1,)), ((), ())), preferred_element_type=F32)


def _inproj_body(x_ref, w_ref, wdt_ref, kt_in_ref, vt_in_ref,
                 q_ref, ktb_ref, vtb_ref, kt_ref, vt_ref, z_ref, xbc_ref, dt_ref):
    del kt_in_ref, vt_in_ref
    x = x_ref[...].astype(BF16)

    def mm(lo, hi):
        return _dot_nt(x, w_ref[lo:hi, :])

    def mm_t(lo, hi):
        return _dot_nt(w_ref[lo:hi, :], x)

    q = mm(0, ATT_W) * QK_SCALE
    for h in range(HEADS):
        q_ref[h] = q[:, h * HEAD_DIM:(h + 1) * HEAD_DIM].astype(BF16)
    kt = mm_t(ATT_W, 2 * ATT_W)
    kt_ref[...] = kt
    ktb_ref[...] = kt.astype(BF16)
    vt = mm_t(2 * ATT_W, 3 * ATT_W)
    vt_ref[...] = vt
    vtb_ref[...] = vt.astype(BF16)
    z_ref[...] = mm(3 * ATT_W, 3 * ATT_W + SSM_W)
    xbc_ref[...] = mm(3 * ATT_W + SSM_W, IN_MAIN)
    dt_ref[...] = _dot_nt(x, wdt_ref[...])


def _inproj(layer, x, w_t, wdt_t, kt_all, vt_all, nseq, tm):
    t = x.shape[0]
    seq = t // nseq
    per_seq = seq // tm
    row = lambda i: (i, 0)
    head = lambda i: (0, i, 0)
    tr = lambda i: (i // per_seq, 0, i % per_seq)
    tr_all = lambda i: (layer, i // per_seq, 0, i % per_seq)
    const = lambda i: (0, 0)
    any_spec = pl.BlockSpec(memory_space=pl.ANY)
    return pl.pallas_call(
        _inproj_body,
        grid=(t // tm,),
        in_specs=[pl.BlockSpec((tm, D_MODEL), row),
                  pl.BlockSpec((IN_MAIN, D_MODEL), const),
                  pl.BlockSpec((LANES, D_MODEL), const),
                  any_spec, any_spec],
        out_specs=[pl.BlockSpec((HEADS, tm, HEAD_DIM), head),
                   pl.BlockSpec((None, ATT_W, tm), tr),
                   pl.BlockSpec((None, ATT_W, tm), tr),
                   pl.BlockSpec((None, None, ATT_W, tm), tr_all),
                   pl.BlockSpec((None, None, ATT_W, tm), tr_all),
                   pl.BlockSpec((tm, SSM_W), row),
                   pl.BlockSpec((tm, CONV_DIM), row),
                   pl.BlockSpec((tm, LANES), row)],
        out_shape=[jax.ShapeDtypeStruct((HEADS, t, HEAD_DIM), BF16),
                   jax.ShapeDtypeStruct((nseq, ATT_W, seq), BF16),
                   jax.ShapeDtypeStruct((nseq, ATT_W, seq), BF16),
                   jax.ShapeDtypeStruct(kt_all.shape, F32),
                   jax.ShapeDtypeStruct(vt_all.shape, F32),
                   jax.ShapeDtypeStruct((t, SSM_W), F32),
                   jax.ShapeDtypeStruct((t, CONV_DIM), F32),
                   jax.ShapeDtypeStruct((t, LANES), F32)],
        input_output_aliases={3: 3, 4: 4},
        compiler_params=_cparams(("arbitrary",)),
        name="inproj",
    )(x, w_t, wdt_t, kt_all, vt_all)


ATTN_SKEW = 2


def _attn_key_tile(bias_ref, q_ref, k_ref, v_ref, u, acc_ref, c_ref, start, mask):
    log_beta, l1m, p = {}, {}, {}
    for k in range(HEADS + 2 * ATTN_SKEW):
        h = k
        if h < HEADS:
            kt = k_ref[h * HEAD_DIM:(h + 1) * HEAD_DIM, pl.ds(start, TK)]
            z = _dot(q_ref[h], kt) + bias_ref[h]
            log_beta[h] = jnp.minimum(z, 0.0) - jnp.log(1.0 + jnp.exp(-jnp.abs(z)))
            l1m[h] = log_beta[h] - z
            if mask is not None:
                l1m[h] = jnp.where(mask, l1m[h], 0.0)
        h = k - ATTN_SKEW
        if 0 <= h < HEADS:
            suffix = _dot(l1m[h].astype(BF16), u)
            ph = jnp.exp(log_beta.pop(h) + suffix)
            if mask is not None:
                ph = jnp.where(mask, ph, 0.0)
            p[h] = ph.astype(BF16)
        h = k - 2 * ATTN_SKEW
        if 0 <= h < HEADS:
            vt = v_ref[h * HEAD_DIM:(h + 1) * HEAD_DIM, pl.ds(start, TK)]
            pv = _dot_nt(p.pop(h), vt)
            dc = jnp.sum(l1m.pop(h), axis=1, keepdims=True)
            if mask is not None:
                acc_ref[h] = pv
                c_ref[h] = dc
            else:
                c = c_ref[h]
                acc_ref[h] += jnp.exp(c) * pv
                c_ref[h] = c + dc


def _attn_body(bias_ref, q_ref, k_ref, v_ref, u_ref, o_ref, acc_ref, c_ref):
    qi = pl.program_id(1)
    u = u_ref[...]
    row = lax.broadcasted_iota(jnp.int32, (TQ, TK), 0)
    col = lax.broadcasted_iota(jnp.int32, (TQ, TK), 1)
    _attn_key_tile(bias_ref, q_ref, k_ref, v_ref, u, acc_ref, c_ref,
                   pl.multiple_of(qi * TK, TK), col < row)

    def step(jj, carry):
        _attn_key_tile(bias_ref, q_ref, k_ref, v_ref, u, acc_ref, c_ref,
                       pl.multiple_of((qi - 1 - jj) * TK, TK), None)
        return carry

    lax.fori_loop(0, qi, step, 0)
    for h in range(HEADS):
        o_ref[:, h * HEAD_DIM:(h + 1) * HEAD_DIM] = acc_ref[h]


def _suffix_matrix(n):
    j = jnp.arange(n)
    return (j[:, None] > j[None, :]).astype(BF16)


def _attn_prompt(q8, kb, vb, bias, bsz, seq):
    nq = seq // TQ
    return pl.pallas_call(
        _attn_body,
        grid_spec=pltpu.PrefetchScalarGridSpec(
            num_scalar_prefetch=1,
            grid=(bsz, nq),
            in_specs=[pl.BlockSpec((HEADS, TQ, HEAD_DIM), lambda b, i, bias: (0, b * nq + i, 0)),
                      pl.BlockSpec((None, ATT_W, seq), lambda b, i, bias: (b, 0, 0)),
                      pl.BlockSpec((None, ATT_W, seq), lambda b, i, bias: (b, 0, 0)),
                      pl.BlockSpec((TK, TK), lambda b, i, bias: (0, 0))],
            out_specs=pl.BlockSpec((TQ, ATT_W), lambda b, i, bias: (b * nq + i, 0)),
            scratch_shapes=[pltpu.VMEM((HEADS, TQ, HEAD_DIM), F32),
                            pltpu.VMEM((HEADS, TQ, 1), F32)]),
        out_shape=jax.ShapeDtypeStruct((bsz * seq, ATT_W), F32),
        compiler_params=_cparams(("arbitrary", "arbitrary")),
        name="attn_prompt",
    )(bias, q8, kb, vb, _suffix_matrix(TK))


def _attn_decode_body(n_pages, layer, pt_ref, q_ref, bias_ref, u_ref, hm_ref, k_hbm, v_hbm,
                      o_ref, kbuf, vbuf, sem):
    b = pl.program_id(0)
    slot = b % 2

    def page_copies(seq_idx, s):
        copies = []
        for p in range(n_pages):
            pg = pt_ref[seq_idx, p]
            copies.append(pltpu.make_async_copy(k_hbm.at[layer, pg], kbuf.at[s, p], sem.at[0, s]))
            copies.append(pltpu.make_async_copy(v_hbm.at[layer, pg], vbuf.at[s, p], sem.at[1, s]))
        return copies

    @pl.when(b == 0)
    def _():
        for cp in page_copies(0, 0):
            cp.start()

    @pl.when(b + 1 < pl.num_programs(0))
    def _():
        for cp in page_copies(b + 1, 1 - slot):
            cp.start()

    for cp in page_copies(b, slot):
        cp.wait()

    q = q_ref[0]
    bias = bias_ref[...]
    u = u_ref[...]
    pages = range(n_pages)
    z = [_dot(q, kbuf[slot, p].astype(BF16)) + bias for p in pages]
    log_beta = [jnp.minimum(zp, 0.0) - jnp.log(1.0 + jnp.exp(-jnp.abs(zp))) for zp in z]
    l1m = [lb - zp for lb, zp in zip(log_beta, z)]
    parts = [jnp.concatenate(_split3(l), axis=0) for l in l1m]
    sums = [_dot(pt, u) for pt in parts]
    nh = 2 * HEADS
    suffix = [s[0:nh] + s[nh:2 * nh] + s[2 * nh:3 * nh] for s in sums]
    w = [jnp.exp(lb + sf).astype(BF16) for lb, sf in zip(log_beta, suffix)]
    pv = [_dot_nt(w[p], vbuf[slot, p].astype(BF16)) for p in pages]
    c = jnp.zeros((nh, 1), F32)
    acc = jnp.zeros((nh, ATT_W), F32)
    for p in reversed(pages):
        acc = acc + jnp.exp(c) * pv[p]
        c = c + jnp.sum(l1m[p], axis=1, keepdims=True)
    o_ref[0] = jnp.sum(acc * hm_ref[...], axis=0, keepdims=True)


def _attn_decode(layer, q_bd, bias_b, cache_k4, cache_v4, page_table):
    nb, n_pages = page_table.shape
    page = cache_k4.shape[3]
    hm = (jnp.arange(ATT_W)[None, :] // HEAD_DIM == jnp.arange(2 * HEADS)[:, None]).astype(F32)
    const = lambda b, pt: (0, 0)
    any_spec = pl.BlockSpec(memory_space=pl.ANY)
    return pl.pallas_call(
        functools.partial(_attn_decode_body, n_pages, layer),
        grid_spec=pltpu.PrefetchScalarGridSpec(
            num_scalar_prefetch=1,
            grid=(nb,),
            in_specs=[pl.BlockSpec((1, 2 * HEADS, ATT_W), lambda b, pt: (b, 0, 0)),
                      pl.BlockSpec((2 * HEADS, page), const),
                      pl.BlockSpec((page, page), const),
                      pl.BlockSpec((2 * HEADS, ATT_W), const),
                      any_spec, any_spec],
            out_specs=pl.BlockSpec((1, 1, ATT_W), lambda b, pt: (b, 0, 0)),
            scratch_shapes=[pltpu.VMEM((2, n_pages, ATT_W, page), F32),
                            pltpu.VMEM((2, n_pages, ATT_W, page), F32),
                            pltpu.SemaphoreType.DMA((2, 2))]),
        out_shape=jax.ShapeDtypeStruct((nb, 1, ATT_W), F32),
        compiler_params=_cparams(("arbitrary",)),
        name="attn_decode",
    )(page_table, q_bd, bias_b, _suffix_matrix(page), hm, cache_k4, cache_v4)


def _ssd_body(xbc_ref, dt_ref, z_ref, cw_ref, cb_ref, dtb_ref, alog_ref, dskip_ref, g_ref, tri_ref,
              y_ref, state_ref, tail_ref, ext_ref, s_ref):
    c = pl.program_id(1)
    nc = pl.num_programs(1)
    q = SSD_CHUNK
    pad = SUBLANES

    @pl.when(c == 0)
    def _():
        ext_ref[0:pad, :] = jnp.zeros((pad, CONV_DIM), F32)
        s_ref[...] = jnp.zeros_like(s_ref)

    @pl.when(c > 0)
    def _():
        ext_ref[0:pad, :] = ext_ref[q:q + pad, :]

    raw = xbc_ref[...]
    ext_ref[pad:pad + q, :] = raw

    @pl.when(c == nc - 1)
    def _():
        tail_ref[0] = raw[q - (CONV_W - 1):, :]

    conv = raw * cw_ref[CONV_W - 1:CONV_W, :] + cb_ref[...]
    for i in range(1, CONV_W):
        conv = conv + ext_ref[pad - i:pad - i + q, :] * cw_ref[CONV_W - 1 - i:CONV_W - i, :]
    xc = conv * _sigmoid(conv)
    xs = xc[:, :SSM_W]
    bm = xc[:, SSM_W:SSM_W + SSM_GROUPS * SSM_STATE]
    cm = xc[:, SSM_W + SSM_GROUPS * SSM_STATE:]

    dt = _softplus(dt_ref[...] + dtb_ref[...])
    a = -jnp.exp(alog_ref[...])
    da = dt * a
    tri = tri_ref[...]
    hi, mid, lo = _split3(da)
    acum = _dot(tri, hi) + _dot(tri, mid) + _dot(tri, lo)
    acum_t = acum.T
    row = lax.broadcasted_iota(jnp.int32, (q, q), 0)
    col = lax.broadcasted_iota(jnp.int32, (q, q), 1)
    causal = col <= row

    y_parts = []
    for g in range(SSM_GROUPS):
        bm_g = bm[:, g * SSM_STATE:(g + 1) * SSM_STATE]
        cm_g = cm[:, g * SSM_STATE:(g + 1) * SSM_STATE].astype(BF16)
        scores = _dot_nt(cm_g, bm_g.astype(BF16))
        gsl = slice(g * HEADS_PER_GROUP * HEAD_DIM, (g + 1) * HEADS_PER_GROUP * HEAD_DIM)
        s_prev = s_ref[gsl, :]
        y_off = _dot_nt(cm_g, s_prev.astype(BF16))
        xdt_parts = []
        for hl in range(HEADS_PER_GROUP):
            h = g * HEADS_PER_GROUP + hl
            hs = slice(h * HEAD_DIM, (h + 1) * HEAD_DIM)
            a_col = acum[:, h:h + 1]
            a_row = acum_t[h:h + 1, :]
            a_last = acum_t[h:h + 1, q - 1:q]
            decay = jnp.where(causal, jnp.exp(a_col - a_row), 0.0)
            xdt = xs[:, hs] * dt[:, h:h + 1]
            y_h = _dot((scores * decay).astype(BF16), xdt.astype(BF16))
            y_h = y_h + y_off[:, hl * HEAD_DIM:(hl + 1) * HEAD_DIM] * jnp.exp(a_col)
            y_parts.append(y_h)
            xdt_parts.append(xdt * jnp.exp(a_last - a_col))
            s_ref[hs, :] = s_ref[hs, :] * jnp.exp(a_last)
        xdt_g = jnp.concatenate(xdt_parts, axis=1)
        s_ref[gsl, :] += _dot(xdt_g.T.astype(BF16), bm_g.astype(BF16))
    y = jnp.concatenate(y_parts, axis=1)

    y = y + xs * dskip_ref[...]
    y = y * (z_ref[...] * _sigmoid(z_ref[...]))
    y = y * lax.rsqrt(jnp.mean(y * y, axis=1, keepdims=True) + NORM_EPS) * g_ref[...]
    y_ref[...] = y.astype(BF16)

    @pl.when(c == nc - 1)
    def _():
        state_ref[0] = s_ref[...].reshape(HEADS, HEAD_DIM, SSM_STATE)


def _row128(v):
    return jnp.pad(v.astype(F32), (0, LANES - v.shape[0]))[None, :]


def _ssd_prompt(xbc, dt_raw, z, conv_w, conv_b, dt_bias, a_log, d_skip, g_ssm, bsz, seq):
    q = SSD_CHUNK
    nc = seq // q
    j = jnp.arange(q)
    tri = (j[None, :] <= j[:, None]).astype(BF16)
    row = lambda b, c: (b * nc + c, 0)
    const = lambda b, c: (0, 0)
    return pl.pallas_call(
        _ssd_body,
        grid=(bsz, nc),
        in_specs=[pl.BlockSpec((q, CONV_DIM), row),
                  pl.BlockSpec((q, LANES), row),
                  pl.BlockSpec((q, SSM_W), row),
                  pl.BlockSpec((CONV_W, CONV_DIM), const),
                  pl.BlockSpec((1, CONV_DIM), const),
                  pl.BlockSpec((1, LANES), const),
                  pl.BlockSpec((1, LANES), const),
                  pl.BlockSpec((1, SSM_W), const),
                  pl.BlockSpec((1, SSM_W), const),
                  pl.BlockSpec((q, q), const)],
        out_specs=[pl.BlockSpec((q, SSM_W), row),
                   pl.BlockSpec((1, HEADS, HEAD_DIM, SSM_STATE), lambda b, c: (b, 0, 0, 0)),
                   pl.BlockSpec((1, CONV_W - 1, CONV_DIM), lambda b, c: (b, 0, 0))],
        out_shape=[jax.ShapeDtypeStruct((bsz * seq, SSM_W), BF16),
                   jax.ShapeDtypeStruct((bsz, HEADS, HEAD_DIM, SSM_STATE), F32),
                   jax.ShapeDtypeStruct((bsz, CONV_W - 1, CONV_DIM), F32)],
        scratch_shapes=[pltpu.VMEM((q + SUBLANES, CONV_DIM), F32),
                        pltpu.VMEM((SSM_W, SSM_STATE), F32)],
        compiler_params=_cparams(("arbitrary", "arbitrary")),
        name="ssd_prompt",
    )(xbc, dt_raw, z, conv_w, conv_b[None, :], _row128(dt_bias), _row128(a_log),
      jnp.repeat(d_skip.astype(F32), HEAD_DIM)[None, :], g_ssm[None, :], tri)


def _ssd_dec_pre_body(xbc_ref, sc_ref, dt_ref, cw_ref, cb_ref, dtb_ref, alog_ref,
                      xs_ref, xdt_ref, bm_ref, cm_ref, dec_ref, cnew_ref):
    raw = xbc_ref[...]
    conv = raw * cw_ref[CONV_W - 1:CONV_W, :] + cb_ref[...]
    for i in range(CONV_W - 1):
        conv = conv + sc_ref[i] * cw_ref[i:i + 1, :]
    for i in range(CONV_W - 2):
        cnew_ref[i] = sc_ref[i + 1]
    cnew_ref[CONV_W - 2] = raw
    xc = conv * _sigmoid(conv)
    xs = xc[:, :SSM_W]
    xs_ref[...] = xs
    bm_ref[...] = xc[:, SSM_W:SSM_W + SSM_GROUPS * SSM_STATE]
    cm_ref[...] = xc[:, SSM_W + SSM_GROUPS * SSM_STATE:]
    dt = _softplus(dt_ref[...] + dtb_ref[...])
    dec_ref[...] = jnp.exp(dt * (-jnp.exp(alog_ref[...])))
    for h in range(HEADS):
        hs = slice(h * HEAD_DIM, (h + 1) * HEAD_DIM)
        xdt_ref[:, hs] = xs[:, hs] * dt[:, h:h + 1]


def _ssd_dec_pre(xbc, state_conv_l, dt_raw, conv_w, conv_b, dt_bias, a_log):
    nb = xbc.shape[0]
    gn = SSM_GROUPS * SSM_STATE
    return pl.pallas_call(
        _ssd_dec_pre_body,
        out_shape=[jax.ShapeDtypeStruct((nb, SSM_W), F32),
                   jax.ShapeDtypeStruct((nb, SSM_W), F32),
                   jax.ShapeDtypeStruct((nb, gn), F32),
                   jax.ShapeDtypeStruct((nb, gn), F32),
                   jax.ShapeDtypeStruct((nb, LANES), F32),
                   jax.ShapeDtypeStruct((CONV_W - 1, nb, CONV_DIM), F32)],
        compiler_params=pltpu.CompilerParams(vmem_limit_bytes=VMEM_LIMIT),
        name="ssd_decode_pre",
    )(xbc, state_conv_l, dt_raw, conv_w, conv_b[None, :], _row128(dt_bias), _row128(a_log))


SEQ_PER_STEP = 8


def _ssd_dec_state_body(dec_ref, xdt_ref, bm_ref, cm_ref, s_ref, snew_ref, y_ref):
    step = pl.program_id(0)
    seqs = range(SEQ_PER_STEP)
    xb = [jnp.broadcast_to(xdt_ref[0, :, i:i + 1], (SSM_W, SSM_STATE)) for i in seqs]
    hn = {}
    for i in seqs:
        b = step * SEQ_PER_STEP + i
        for h in range(HEADS):
            g = h // HEADS_PER_GROUP
            hs = slice(h * HEAD_DIM, (h + 1) * HEAD_DIM)
            brow = bm_ref[0, i:i + 1, g * SSM_STATE:(g + 1) * SSM_STATE]
            hn[i, h] = s_ref[i, h] * dec_ref[b, h] + xb[i][hs, :] * brow
            snew_ref[i, h] = hn[i, h]
    ys = {}
    for i in seqs:
        for h in range(HEADS):
            g = h // HEADS_PER_GROUP
            crow = cm_ref[0, i:i + 1, g * SSM_STATE:(g + 1) * SSM_STATE]
            ys[i, h] = jnp.sum(hn[i, h] * crow, axis=1, keepdims=True)
    for i in seqs:
        y_ref[0, :, i:i + 1] = jnp.concatenate([ys[i, h] for h in range(HEADS)], axis=0)


def _ssd_dec_state(dec, xdt_t, bm3, cm3, state_l):
    nb = state_l.shape[0]
    steps = nb // SEQ_PER_STEP
    gn = SSM_GROUPS * SSM_STATE
    blk3 = lambda s: (s, 0, 0)
    blk4 = lambda s: (s, 0, 0, 0)
    return pl.pallas_call(
        _ssd_dec_state_body,
        grid=(steps,),
        in_specs=[pl.BlockSpec(memory_space=pltpu.SMEM),
                  pl.BlockSpec((1, SSM_W, SEQ_PER_STEP), blk3),
                  pl.BlockSpec((1, SEQ_PER_STEP, gn), blk3),
                  pl.BlockSpec((1, SEQ_PER_STEP, gn), blk3),
                  pl.BlockSpec((SEQ_PER_STEP, HEADS, HEAD_DIM, SSM_STATE), blk4)],
        out_specs=[pl.BlockSpec((SEQ_PER_STEP, HEADS, HEAD_DIM, SSM_STATE), blk4),
                   pl.BlockSpec((1, SSM_W, SEQ_PER_STEP), blk3)],
        out_shape=[jax.ShapeDtypeStruct(state_l.shape, F32),
                   jax.ShapeDtypeStruct((steps, SSM_W, SEQ_PER_STEP), F32)],
        compiler_params=_cparams(("arbitrary",)),
        name="ssd_decode_state",
    )(dec, xdt_t, bm3, cm3, state_l)


def _gate_norm_body(y_ref, xs_ref, z_ref, dskip_ref, g_ref, o_ref):
    y = y_ref[...] + xs_ref[...] * dskip_ref[...]
    z = z_ref[...]
    y = y * (z * _sigmoid(z))
    y = y * lax.rsqrt(jnp.mean(y * y, axis=1, keepdims=True) + NORM_EPS) * g_ref[...]
    o_ref[...] = y.astype(BF16)


def _gate_norm(y, xs, z, d_skip, g_ssm):
    return pl.pallas_call(
        _gate_norm_body,
        out_shape=jax.ShapeDtypeStruct(y.shape, BF16),
        name="ssd_decode_gate",
    )(y, xs, z, jnp.repeat(d_skip.astype(F32), HEAD_DIM)[None, :], g_ssm[None, :])


def _layer_norm(u, g, b):
    mu = jnp.mean(u, axis=1, keepdims=True)
    d = u - mu
    var = jnp.mean(d * d, axis=1, keepdims=True)
    return d * lax.rsqrt(var + NORM_EPS) * g + b


def _outproj_body(att_ref, yg_ref, h_ref, wo_ref, gatt_ref, g1_ref, b1_ref, wr_ref, br_ref,
                  h1_ref, lg_ref):
    att = att_ref[...]
    att = att * lax.rsqrt(jnp.mean(att * att, axis=1, keepdims=True) + NORM_EPS) * gatt_ref[...]
    mix = _dot(att.astype(BF16), wo_ref[0:ATT_W, :]) + _dot(yg_ref[...], wo_ref[ATT_W:, :])
    h1 = _layer_norm(DN_ALPHA * h_ref[...] + mix, g1_ref[...], b1_ref[...])
    h1_ref[...] = h1
    lg_ref[...] = _dot_nt(h1.astype(BF16), wr_ref[...]) + br_ref[...]


def _outproj(att, yg, h, w_out_b, g_att, g1, b1, w_router_p, b_router_p, tm):
    t = att.shape[0]
    row = lambda i: (i, 0)
    const = lambda i: (0, 0)
    in_specs = [pl.BlockSpec((tm, ATT_W), row),
                pl.BlockSpec((tm, SSM_W), row),
                pl.BlockSpec((tm, D_MODEL), row),
                pl.BlockSpec((D_MODEL, D_MODEL), const),
                pl.BlockSpec((1, ATT_W), const),
                pl.BlockSpec((1, D_MODEL), const),
                pl.BlockSpec((1, D_MODEL), const),
                pl.BlockSpec((LANES, D_MODEL), const),
                pl.BlockSpec((1, LANES), const)]
    args = [att, yg, h, w_out_b, g_att[None, :], g1[None, :], b1[None, :], w_router_p, b_router_p]
    return pl.pallas_call(
        _outproj_body,
        grid=(t // tm,),
        in_specs=in_specs,
        out_specs=[pl.BlockSpec((tm, D_MODEL), row), pl.BlockSpec((tm, LANES), row)],
        out_shape=[jax.ShapeDtypeStruct((t, D_MODEL), F32),
                   jax.ShapeDtypeStruct((t, LANES), F32)],
        compiler_params=_cparams(("arbitrary",)),
        name="outproj",
    )(*args)


def _row_permute_body(tm, gather, idx_ref, *refs):
    if gather:
        src_ref, blk_ref, sem = refs
    else:
        blk_ref, _, dst_ref, sem = refs
    base = pl.program_id(0) * tm

    def copy(r):
        j = idx_ref[base + r]
        if gather:
            return pltpu.make_async_copy(src_ref.at[pl.ds(j, 1)], blk_ref.at[pl.ds(r, 1)], sem)
        return pltpu.make_async_copy(blk_ref.at[pl.ds(r, 1)], dst_ref.at[pl.ds(j, 1)], sem)

    def start(r, carry):
        copy(r).start()
        return carry

    def wait(r, carry):
        copy(r).wait()
        return carry

    lax.fori_loop(0, tm, start, 0, unroll=8)
    lax.fori_loop(0, tm, wait, 0, unroll=8)


def _row_permute(idx, src, dst, gather, tm):
    n_rows, d = idx.shape[0], src.shape[1]
    any_spec = pl.BlockSpec(memory_space=pl.ANY)
    blk_spec = pl.BlockSpec((tm, d), lambda i, idx: (i, 0))
    if gather:
        args, in_specs, out_specs, aliases = (idx, src), [any_spec], blk_spec, {}
        out_shape = jax.ShapeDtypeStruct((n_rows, d), src.dtype)
    else:
        args, in_specs, out_specs, aliases = (idx, src, dst), [blk_spec, any_spec], any_spec, {2: 0}
        out_shape = jax.ShapeDtypeStruct(dst.shape, dst.dtype)
    return pl.pallas_call(
        functools.partial(_row_permute_body, tm, gather),
        grid_spec=pltpu.PrefetchScalarGridSpec(
            num_scalar_prefetch=1,
            grid=(n_rows // tm,),
            in_specs=in_specs,
            out_specs=out_specs,
            scratch_shapes=[pltpu.SemaphoreType.DMA]),
        out_shape=out_shape,
        input_output_aliases=aliases,
        compiler_params=_cparams(("arbitrary",)),
        name="row_gather" if gather else "row_scatter",
    )(*args)


def _moe_body(lo_ref, hi_ref, first_ref, nact_ref,
              x_ref, wr_ref, br_ref, g2_ref, b2_ref,
              wg_lo, wu_lo, wd_lo, wg_hi, wu_hi, wd_hi,
              o_ref, wgb, wub, wdb):
    i = pl.program_id(0)

    @pl.when(i < nact_ref[0])
    def _():
        @pl.when(first_ref[i] == 1)
        def _():
            wgb[0] = wg_lo[...].astype(BF16)
            wub[0] = wu_lo[...].astype(BF16)
            wdb[0] = wd_lo[...].astype(BF16)
            wgb[1] = wg_hi[...].astype(BF16)
            wub[1] = wu_hi[...].astype(BF16)
            wdb[1] = wd_hi[...].astype(BF16)

        x = x_ref[...]
        xb = x.astype(BF16)
        logits = _dot_nt(xb, wr_ref[...]) + br_ref[...]
        lane = lax.broadcasted_iota(jnp.int32, logits.shape, 1)
        valid = lane < N_EXPERTS
        m = jnp.max(jnp.where(valid, logits, -jnp.inf), axis=1, keepdims=True)
        e = jnp.where(valid, jnp.exp(logits - m), 0.0)
        probs = e / jnp.sum(e, axis=1, keepdims=True)
        p_lo = jnp.sum(jnp.where(lane == lo_ref[i], probs, 0.0), axis=1, keepdims=True)
        p_hi = jnp.sum(jnp.where(lane == hi_ref[i], probs, 0.0), axis=1, keepdims=True)
        denom = p_lo + p_hi
        out = jnp.zeros_like(x)
        for slot, gate in ((0, p_lo / denom), (1, p_hi / denom)):
            hg = _dot(xb, wgb[slot])
            hu = _dot(xb, wub[slot])
            hmid = (hg * _sigmoid(hg)) * hu
            out = out + gate * _dot(hmid.astype(BF16), wdb[slot])
        o_ref[...] = _layer_norm(DN_ALPHA * x + out, g2_ref[...], b2_ref[...])

    @pl.when(i >= nact_ref[0])
    def _():
        o_ref[...] = jnp.zeros_like(o_ref)


def _moe_sorted(layer, x_sorted, tile_lo, tile_hi, tile_first, n_active,
                w_router_p, b_router_p, g2, b2, w_gate, w_up, w_down):
    n_tiles = tile_lo.shape[0]
    tm = TM_MOE

    def xmap(i, lo, hi, first, nact):
        return (jnp.minimum(i, nact[0] - 1), 0)

    const = lambda i, lo, hi, first, nact: (0, 0)
    w_lo = lambda i, lo, hi, first, nact: (layer, lo[i], 0, 0)
    w_hi = lambda i, lo, hi, first, nact: (layer, hi[i], 0, 0)
    up_blk = (None, None, D_MODEL, D_FF)
    dn_blk = (None, None, D_FF, D_MODEL)
    return pl.pallas_call(
        _moe_body,
        grid_spec=pltpu.PrefetchScalarGridSpec(
            num_scalar_prefetch=4,
            grid=(n_tiles,),
            in_specs=[pl.BlockSpec((tm, D_MODEL), xmap),
                      pl.BlockSpec((LANES, D_MODEL), const),
                      pl.BlockSpec((1, LANES), const),
                      pl.BlockSpec((1, D_MODEL), const),
                      pl.BlockSpec((1, D_MODEL), const),
                      pl.BlockSpec(up_blk, w_lo), pl.BlockSpec(up_blk, w_lo), pl.BlockSpec(dn_blk, w_lo),
                      pl.BlockSpec(up_blk, w_hi), pl.BlockSpec(up_blk, w_hi), pl.BlockSpec(dn_blk, w_hi)],
            out_specs=pl.BlockSpec((tm, D_MODEL), lambda i, lo, hi, first, nact: (i, 0)),
            scratch_shapes=[pltpu.VMEM((2, D_MODEL, D_FF), BF16),
                            pltpu.VMEM((2, D_MODEL, D_FF), BF16),
                            pltpu.VMEM((2, D_FF, D_MODEL), BF16)]),
        out_shape=jax.ShapeDtypeStruct(x_sorted.shape, F32),
        compiler_params=_cparams(("arbitrary",)),
        name="moe",
    )(tile_lo, tile_hi, tile_first, n_active,
      x_sorted, w_router_p, b_router_p, g2[None, :], b2[None, :],
      w_gate, w_up, w_down, w_gate, w_up, w_down)


def _route(logits, n_tiles):
    t = logits.shape[0]
    probs = jax.nn.softmax(logits, axis=-1)
    grouped = probs.reshape(t, N_EXPERT_GROUPS, EXPERTS_PER_GROUP)
    g_sel = jnp.argmax(grouped.max(-1), axis=-1)
    in_group = jnp.take_along_axis(grouped, g_sel[:, None, None], axis=1)[:, 0, :]
    _, top_i = lax.top_k(in_group, 2)
    e_lo = jnp.min(top_i, axis=-1)
    e_hi = jnp.max(top_i, axis=-1)
    pair = e_lo * (2 * EXPERTS_PER_GROUP - 1 - e_lo) // 2 + (e_hi - e_lo - 1)
    bucket = (g_sel * N_PAIRS + pair).astype(jnp.int32)

    onehot = (bucket[:, None] == jnp.arange(N_BUCKETS, dtype=jnp.int32)[None, :]).astype(jnp.int32)
    rank = jnp.sum((jnp.cumsum(onehot, axis=0) - onehot) * onehot, axis=1)
    counts = jnp.sum(onehot, axis=0)
    tiles_b = (counts + TM_MOE - 1) // TM_MOE
    tile_end = jnp.cumsum(tiles_b)
    tile_start = tile_end - tiles_b
    pos = (tile_start[bucket] * TM_MOE + rank).astype(jnp.int32)
    n_active = tile_end[-1]

    tile_id = jnp.minimum(jnp.arange(n_tiles, dtype=jnp.int32), n_active - 1)
    tile_bucket = jnp.sum((tile_id[:, None] >= tile_end[None, :]).astype(jnp.int32), axis=1)
    grp = tile_bucket // N_PAIRS
    pr = tile_bucket % N_PAIRS
    pair_lo = jnp.array([0, 0, 0, 1, 1, 2], jnp.int32)
    pair_hi = jnp.array([1, 2, 3, 2, 3, 3], jnp.int32)
    tile_lo = (grp * EXPERTS_PER_GROUP + pair_lo[pr]).astype(jnp.int32)
    tile_hi = (grp * EXPERTS_PER_GROUP + pair_hi[pr]).astype(jnp.int32)
    prev = jnp.concatenate([jnp.full((1,), -1, jnp.int32), tile_bucket[:-1].astype(jnp.int32)])
    tile_first = (tile_bucket != prev).astype(jnp.int32)
    return pos, tile_lo, tile_hi, tile_first, n_active.astype(jnp.int32)[None]


def kernel(x_prompt, x_sample, cache_k, cache_v, state_ssm, state_conv, page_table, w_in, att_bias,
           conv_w, conv_b, dt_bias, a_log, d_skip, attn_norm_g, ssm_norm_g, w_out, ln1_g, ln1_b,
           ln2_g, ln2_b, w_router, b_router, w_gate, w_up, w_down):
    bsz, seq, _ = x_prompt.shape
    nb = x_sample.shape[0]
    tp = bsz * seq
    t_all = tp + nb
    n_tiles = t_all // TM_MOE + N_BUCKETS
    pool, page = cache_k.shape[1], cache_k.shape[2]

    hp = x_prompt.reshape(tp, D_MODEL)
    hs = x_sample.reshape(nb, D_MODEL)
    cache_k4 = cache_k.transpose(0, 1, 3, 4, 2).reshape(DEPTH, pool, ATT_W, page)
    cache_v4 = cache_v.transpose(0, 1, 3, 4, 2).reshape(DEPTH, pool, ATT_W, page)
    state_conv_t = state_conv.transpose(0, 2, 1, 3)
    w_in_t = w_in.transpose(0, 2, 1).astype(BF16)
    w_main_t = w_in_t[:, :IN_MAIN, :]
    w_dt_t = jnp.pad(w_in_t[:, IN_MAIN:, :], ((0, 0), (0, LANES - HEADS), (0, 0)))
    w_out_b = w_out.astype(BF16)
    w_router_p = jnp.pad(w_router.T.astype(BF16), ((0, LANES - N_EXPERTS), (0, 0)))
    b_router_p = jnp.pad(b_router.astype(F32), (0, LANES - N_EXPERTS))[None, :]
    head_of_row = jnp.arange(2 * HEADS)[:, None]
    head_of_lane = jnp.arange(ATT_W)[None, :] // HEAD_DIM

    k_p = jnp.zeros((DEPTH, bsz, ATT_W, seq), F32)
    v_p = jnp.zeros((DEPTH, bsz, ATT_W, seq), F32)
    k_s = jnp.zeros((DEPTH, 1, ATT_W, nb), F32)
    v_s = jnp.zeros((DEPTH, 1, ATT_W, nb), F32)

    outs = [[] for _ in range(4)]
    for l in range(DEPTH):
        q8, kb, vb, k_p, v_p, z_p, xbc_p, dt_p = _inproj(
            l, hp, w_main_t[l], w_dt_t[l], k_p, v_p, bsz, TM_PROJ)
        att_p = _attn_prompt(q8, kb, vb, att_bias[l].astype(F32), bsz, seq)
        yg_p, ssm_p, conv_p = _ssd_prompt(xbc_p, dt_p, z_p, conv_w[l], conv_b[l], dt_bias[l], a_log[l],
                                          d_skip[l], ssm_norm_g[l], bsz, seq)
        q8s, _, _, k_s, v_s, z_s, xbc_s, dt_s = _inproj(
            l, hs, w_main_t[l], w_dt_t[l], k_s, v_s, 1, nb)
        q_rows = q8s.transpose(1, 0, 2).reshape(nb, 1, ATT_W)
        q_bd = jnp.where(head_of_row[None] == head_of_lane[None], q_rows, 0).astype(BF16)
        bias_b = jnp.broadcast_to(
            jnp.pad(att_bias[l].astype(F32), (0, HEADS))[:, None], (2 * HEADS, page))
        att_s = _attn_decode(l, q_bd, bias_b, cache_k4, cache_v4, page_table).reshape(nb, ATT_W)
        xs_s, xdt_s, bm_s, cm_s, dec_s, conv_s = _ssd_dec_pre(
            xbc_s, state_conv_t[l], dt_s, conv_w[l], conv_b[l], dt_bias[l], a_log[l])
        steps = nb // SEQ_PER_STEP
        xdt_t = xdt_s.reshape(steps, SEQ_PER_STEP, SSM_W).transpose(0, 2, 1)
        ssm_s, y_t = _ssd_dec_state(dec_s, xdt_t,
                                    bm_s.reshape(steps, SEQ_PER_STEP, -1),
                                    cm_s.reshape(steps, SEQ_PER_STEP, -1), state_ssm[l])
        y_s = y_t.transpose(0, 2, 1).reshape(nb, SSM_W)
        yg_s = _gate_norm(y_s, xs_s, z_s, d_skip[l], ssm_norm_g[l])
        h1_p, lg_p = _outproj(att_p, yg_p, hp, w_out_b[l], attn_norm_g[l], ln1_g[l], ln1_b[l],
                              w_router_p, b_router_p, TM_PROJ)
        h1_s, lg_s = _outproj(att_s, yg_s, hs, w_out_b[l], attn_norm_g[l], ln1_g[l], ln1_b[l],
                              w_router_p, b_router_p, nb)
        logits = jnp.concatenate([lg_p[:, :N_EXPERTS], lg_s[:, :N_EXPERTS]], axis=0)
        pos, tile_lo, tile_hi, tile_first, n_active = _route(logits, n_tiles)
        x_sorted = jnp.zeros((n_tiles * TM_MOE, D_MODEL), F32)
        x_sorted = _row_permute(pos[:tp], h1_p, x_sorted, False, TM_PERM)
        x_sorted = _row_permute(pos[tp:], h1_s, x_sorted, False, nb)
        y_sorted = _moe_sorted(l, x_sorted, tile_lo, tile_hi, tile_first, n_active,
                               w_router_p, b_router_p, ln2_g[l], ln2_b[l], w_gate, w_up, w_down)
        hp = _row_permute(pos[:tp], y_sorted, None, True, TM_PERM)
        hs = _row_permute(pos[tp:], y_sorted, None, True, nb)

        for lst, val in zip(outs, (ssm_p, conv_p, ssm_s, conv_s)):
            lst.append(val)

    ssm_p, conv_p, ssm_s, conv_s = [jnp.stack(o) for o in outs]

    def untranspose(kt, n, length):
        return kt.reshape(DEPTH, n, HEADS, HEAD_DIM, length).transpose(0, 1, 4, 2, 3)

    return (hp.reshape(bsz, seq, D_MODEL),
            hs.reshape(nb, 1, D_MODEL),
            untranspose(k_p, bsz, seq),
            untranspose(v_p, bsz, seq),
            ssm_p,
            conv_p,
            untranspose(k_s, 1, nb).transpose(0, 2, 1, 3, 4),
            untranspose(v_s, 1, nb).transpose(0, 2, 1, 3, 4),
            ssm_s,
            conv_s.transpose(0, 2, 1, 3))
```

```python
import functools
import math

import jax
import jax.numpy as jnp
from jax import lax
from jax.experimental import pallas as pl
from jax.experimental.pallas import tpu as pltpu

F32 = jnp.float32
BF16 = jnp.bfloat16

D_MODEL = 1024
DEPTH = 4
HEADS = 8
HEAD_DIM = 64
ATT_W = HEADS * HEAD_DIM
SSM_W = HEADS * HEAD_DIM
SSM_STATE = 128
SSM_GROUPS = 2
HEADS_PER_GROUP = HEADS // SSM_GROUPS
CONV_W = 4
CONV_DIM = SSM_W + 2 * SSM_GROUPS * SSM_STATE
IN_MAIN = 3 * ATT_W + SSM_W + CONV_DIM
N_EXPERTS = 16
N_EXPERT_GROUPS = 4
EXPERTS_PER_GROUP = 4
N_PAIRS = 6
N_BUCKETS = N_EXPERT_GROUPS * N_PAIRS
D_FF = D_MODEL // 2
DN_ALPHA = (2 * DEPTH) ** 0.25
NORM_EPS = 1e-5
QK_SCALE = HEAD_DIM ** -0.5

LANES = 128
SUBLANES = 8
MXU_DIM = 256
VMEM_LIMIT = 56 * 1024 * 1024

TM_PROJ = 512
TQ = MXU_DIM
TK = MXU_DIM
SSD_CHUNK = 128
TM_MOE = 256
TM_PERM = 512


def _cparams(sem):
    return pltpu.CompilerParams(dimension_semantics=sem, vmem_limit_bytes=VMEM_LIMIT)


def _sigmoid(x):
    return 1.0 / (1.0 + jnp.exp(-x))


def _softplus(x):
    return jnp.maximum(x, 0.0) + jnp.log(1.0 + jnp.exp(-jnp.abs(x)))


def _split3(x):
    hi = x.astype(BF16)
    r = x - hi.astype(F32)
    mid = r.astype(BF16)
    lo = (r - mid.astype(F32)).astype(BF16)
    return hi, mid, lo


def _dot(a, b):
    return jnp.dot(a, b, preferred_element_type=F32)


def _dot_nt(a, b):
    return lax.dot_general(a, b, (((1,), (1,)), ((), ())), preferred_element_type=F32)


def _inproj_body(x_ref, w_ref, wdt_ref, kt_in_ref, vt_in_ref,
                 q_ref, ktb_ref, vtb_ref, kt_ref, vt_ref, z_ref, xbc_ref, dt_ref):
    del kt_in_ref, vt_in_ref
    x = x_ref[...].astype(BF16)

    def mm(lo, hi):
        return _dot_nt(x, w_ref[lo:hi, :])

    def mm_t(lo, hi):
        return _dot_nt(w_ref[lo:hi, :], x)

    q = mm(0, ATT_W) * QK_SCALE
    for h in range(HEADS):
        q_ref[h] = q[:, h * HEAD_DIM:(h + 1) * HEAD_DIM].astype(BF16)
    kt = mm_t(ATT_W, 2 * ATT_W)
    kt_ref[...] = kt
    ktb_ref[...] = kt.astype(BF16)
    vt = mm_t(2 * ATT_W, 3 * ATT_W)
    vt_ref[...] = vt
    vtb_ref[...] = vt.astype(BF16)
    z_ref[...] = mm(3 * ATT_W, 3 * ATT_W + SSM_W)
    xbc_ref[...] = mm(3 * ATT_W + SSM_W, IN_MAIN)
    dt_ref[...] = _dot_nt(x, wdt_ref[...])


def _inproj(layer, x, w_t, wdt_t, kt_all, vt_all, nseq, tm):
    t = x.shape[0]
    seq = t // nseq
    per_seq = seq // tm
    row = lambda i: (i, 0)
    head = lambda i: (0, i, 0)
    tr = lambda i: (i // per_seq, 0, i % per_seq)
    tr_all = lambda i: (layer, i // per_seq, 0, i % per_seq)
    const = lambda i: (0, 0)
    any_spec = pl.BlockSpec(memory_space=pl.ANY)
    return pl.pallas_call(
        _inproj_body,
        grid=(t // tm,),
        in_specs=[pl.BlockSpec((tm, D_MODEL), row),
                  pl.BlockSpec((IN_MAIN, D_MODEL), const),
                  pl.BlockSpec((LANES, D_MODEL), const),
                  any_spec, any_spec],
        out_specs=[pl.BlockSpec((HEADS, tm, HEAD_DIM), head),
                   pl.BlockSpec((None, ATT_W, tm), tr),
                   pl.BlockSpec((None, ATT_W, tm), tr),
                   pl.BlockSpec((None, None, ATT_W, tm), tr_all),
                   pl.BlockSpec((None, None, ATT_W, tm), tr_all),
                   pl.BlockSpec((tm, SSM_W), row),
                   pl.BlockSpec((tm, CONV_DIM), row),
                   pl.BlockSpec((tm, LANES), row)],
        out_shape=[jax.ShapeDtypeStruct((HEADS, t, HEAD_DIM), BF16),
                   jax.ShapeDtypeStruct((nseq, ATT_W, seq), BF16),
                   jax.ShapeDtypeStruct((nseq, ATT_W, seq), BF16),
                   jax.ShapeDtypeStruct(kt_all.shape, F32),
                   jax.ShapeDtypeStruct(vt_all.shape, F32),
                   jax.ShapeDtypeStruct((t, SSM_W), F32),
                   jax.ShapeDtypeStruct((t, CONV_DIM), F32),
                   jax.ShapeDtypeStruct((t, LANES), F32)],
        input_output_aliases={3: 3, 4: 4},
        compiler_params=_cparams(("arbitrary",)),
        name="inproj",
    )(x, w_t, wdt_t, kt_all, vt_all)


ATTN_SKEW = 1


def _attn_key_tile(bias_ref, q_ref, k_ref, v_ref, u, acc_ref, c_ref, start, mask):
    log_beta, l1m, p = {}, {}, {}
    for k in range(HEADS + 2 * ATTN_SKEW):
        h = k
        if h < HEADS:
            kt = k_ref[h * HEAD_DIM:(h + 1) * HEAD_DIM, pl.ds(start, TK)]
            z = _dot(q_ref[h], kt) + bias_ref[h]
            log_beta[h] = jnp.minimum(z, 0.0) - jnp.log(1.0 + jnp.exp(-jnp.abs(z)))
            l1m[h] = log_beta[h] - z
            if mask is not None:
                l1m[h] = jnp.where(mask, l1m[h], 0.0)
        h = k - ATTN_SKEW
        if 0 <= h < HEADS:
            suffix = _dot(l1m[h].astype(BF16), u)
            ph = jnp.exp(log_beta.pop(h) + suffix)
            if mask is not None:
                ph = jnp.where(mask, ph, 0.0)
            p[h] = ph.astype(BF16)
        h = k - 2 * ATTN_SKEW
        if 0 <= h < HEADS:
            vt = v_ref[h * HEAD_DIM:(h + 1) * HEAD_DIM, pl.ds(start, TK)]
            pv = _dot_nt(p.pop(h), vt)
            dc = jnp.sum(l1m.pop(h), axis=1, keepdims=True)
            if mask is not None:
                acc_ref[h] = pv
                c_ref[h] = dc
            else:
                c = c_ref[h]
                acc_ref[h] += jnp.exp(c) * pv
                c_ref[h] = c + dc


def _suffix_matrix(n):
    j = jnp.arange(n)
    return (j[:, None] > j[None, :]).astype(BF16)


def _decode_sequence(n_pages, q, bias, u, hm, kbuf, vbuf, slot):
    pages = range(n_pages)
    z = [_dot(q, kbuf[slot, p].astype(BF16)) + bias for p in pages]
    log_beta = [jnp.minimum(zp, 0.0) - jnp.log(1.0 + jnp.exp(-jnp.abs(zp))) for zp in z]
    l1m = [lb - zp for lb, zp in zip(log_beta, z)]
    parts = [jnp.concatenate(_split3(l), axis=0) for l in l1m]
    sums = [_dot(pt, u) for pt in parts]
    nh = 2 * HEADS
    suffix = [s[0:nh] + s[nh:2 * nh] + s[2 * nh:3 * nh] for s in sums]
    w = [jnp.exp(lb + sf).astype(BF16) for lb, sf in zip(log_beta, suffix)]
    pv = [_dot_nt(w[p], vbuf[slot, p].astype(BF16)) for p in pages]
    c = jnp.zeros((nh, 1), F32)
    acc = jnp.zeros((nh, ATT_W), F32)
    for p in reversed(pages):
        acc = acc + jnp.exp(c) * pv[p]
        c = c + jnp.sum(l1m[p], axis=1, keepdims=True)
    return jnp.sum(acc * hm, axis=0, keepdims=True)


SEQ_PER_ATTN_STEP = 2


def _attn_body(n_pages, layer, bias_ref, pt_ref,
               q_ref, k_ref, v_ref, u_ref, qd_ref, biasd_ref, ud_ref, hm_ref, k_hbm, v_hbm,
               o_ref, od_ref, acc_ref, c_ref, kbuf, vbuf, sem):
    qi = pl.program_id(1)
    step = pl.program_id(0) * pl.num_programs(1) + qi
    n_steps = pl.num_programs(0) * pl.num_programs(1)
    seq0 = step * SEQ_PER_ATTN_STEP

    def page_copies(seq_idx, s):
        copies = []
        for p in range(n_pages):
            pg = pt_ref[seq_idx, p]
            copies.append(pltpu.make_async_copy(k_hbm.at[layer, pg], kbuf.at[s, p], sem.at[0, s]))
            copies.append(pltpu.make_async_copy(v_hbm.at[layer, pg], vbuf.at[s, p], sem.at[1, s]))
        return copies

    def decode(j):
        for cp in page_copies(seq0 + j, j):
            cp.wait()
        od_ref[j] = _decode_sequence(n_pages, qd_ref[j], biasd_ref[...], ud_ref[...], hm_ref[...],
                                     kbuf, vbuf, j)

    @pl.when(step == 0)
    def _():
        for cp in page_copies(0, 0):
            cp.start()

    for cp in page_copies(seq0 + 1, 1):
        cp.start()
    decode(0)

    @pl.when(step + 1 < n_steps)
    def _():
        for cp in page_copies(seq0 + SEQ_PER_ATTN_STEP, 0):
            cp.start()

    u = u_ref[...]
    row = lax.broadcasted_iota(jnp.int32, (TQ, TK), 0)
    col = lax.broadcasted_iota(jnp.int32, (TQ, TK), 1)
    _attn_key_tile(bias_ref, q_ref, k_ref, v_ref, u, acc_ref, c_ref,
                   pl.multiple_of(qi * TK, TK), col < row)

    def key_step(jj, carry):
        _attn_key_tile(bias_ref, q_ref, k_ref, v_ref, u, acc_ref, c_ref,
                       pl.multiple_of((qi - 1 - jj) * TK, TK), None)
        return carry

    lax.fori_loop(0, qi, key_step, 0)
    for h in range(HEADS):
        o_ref[:, h * HEAD_DIM:(h + 1) * HEAD_DIM] = acc_ref[h]

    decode(1)


def _attention(layer, q8, kb, vb, bias, bsz, seq, q_bd, bias_b, cache_k4, cache_v4, page_table):
    nq = seq // TQ
    nb, n_pages = page_table.shape
    page = cache_k4.shape[3]
    assert nb == SEQ_PER_ATTN_STEP * bsz * nq
    hm = (jnp.arange(ATT_W)[None, :] // HEAD_DIM == jnp.arange(2 * HEADS)[:, None]).astype(F32)
    const = lambda b, i, bias, pt: (0, 0)
    per_step = lambda b, i, bias, pt: (b * nq + i, 0, 0)
    any_spec = pl.BlockSpec(memory_space=pl.ANY)
    return pl.pallas_call(
        functools.partial(_attn_body, n_pages, layer),
        grid_spec=pltpu.PrefetchScalarGridSpec(
            num_scalar_prefetch=2,
            grid=(bsz, nq),
            in_specs=[pl.BlockSpec((HEADS, TQ, HEAD_DIM), lambda b, i, bias, pt: (0, b * nq + i, 0)),
                      pl.BlockSpec((None, ATT_W, seq), lambda b, i, bias, pt: (b, 0, 0)),
                      pl.BlockSpec((None, ATT_W, seq), lambda b, i, bias, pt: (b, 0, 0)),
                      pl.BlockSpec((TK, TK), const),
                      pl.BlockSpec((SEQ_PER_ATTN_STEP, 2 * HEADS, ATT_W), per_step),
                      pl.BlockSpec((2 * HEADS, page), const),
                      pl.BlockSpec((page, page), const),
                      pl.BlockSpec((2 * HEADS, ATT_W), const),
                      any_spec, any_spec],
            out_specs=[pl.BlockSpec((TQ, ATT_W), lambda b, i, bias, pt: (b * nq + i, 0)),
                       pl.BlockSpec((SEQ_PER_ATTN_STEP, 1, ATT_W), per_step)],
            scratch_shapes=[pltpu.VMEM((HEADS, TQ, HEAD_DIM), F32),
                            pltpu.VMEM((HEADS, TQ, 1), F32),
                            pltpu.VMEM((SEQ_PER_ATTN_STEP, n_pages, ATT_W, page), F32),
                            pltpu.VMEM((SEQ_PER_ATTN_STEP, n_pages, ATT_W, page), F32),
                            pltpu.SemaphoreType.DMA((2, SEQ_PER_ATTN_STEP))]),
        out_shape=[jax.ShapeDtypeStruct((bsz * seq, ATT_W), F32),
                   jax.ShapeDtypeStruct((nb, 1, ATT_W), F32)],
        compiler_params=_cparams(("arbitrary", "arbitrary")),
        name="attention",
    )(bias, page_table, q8, kb, vb, _suffix_matrix(TK),
      q_bd, bias_b, _suffix_matrix(page), hm, cache_k4, cache_v4)


def _ssd_body(xbc_ref, dt_ref, z_ref, cw_ref, cb_ref, dtb_ref, alog_ref, dskip_ref, g_ref, tri_ref,
              y_ref, state_ref, tail_ref, ext_ref, s_ref):
    c = pl.program_id(1)
    nc = pl.num_programs(1)
    q = SSD_CHUNK
    pad = SUBLANES

    @pl.when(c == 0)
    def _():
        ext_ref[0:pad, :] = jnp.zeros((pad, CONV_DIM), F32)
        s_ref[...] = jnp.zeros_like(s_ref)

    @pl.when(c > 0)
    def _():
        ext_ref[0:pad, :] = ext_ref[q:q + pad, :]

    raw = xbc_ref[...]
    ext_ref[pad:pad + q, :] = raw

    @pl.when(c == nc - 1)
    def _():
        tail_ref[0] = raw[q - (CONV_W - 1):, :]

    conv = raw * cw_ref[CONV_W - 1:CONV_W, :] + cb_ref[...]
    for i in range(1, CONV_W):
        conv = conv + ext_ref[pad - i:pad - i + q, :] * cw_ref[CONV_W - 1 - i:CONV_W - i, :]
    xc = conv * _sigmoid(conv)
    xs = xc[:, :SSM_W]
    bm = xc[:, SSM_W:SSM_W + SSM_GROUPS * SSM_STATE]
    cm = xc[:, SSM_W + SSM_GROUPS * SSM_STATE:]

    dt = _softplus(dt_ref[...] + dtb_ref[...])
    a = -jnp.exp(alog_ref[...])
    da = dt * a
    tri = tri_ref[...]
    hi, mid, lo = _split3(da)
    acum = _dot(tri, hi) + _dot(tri, mid) + _dot(tri, lo)
    acum_t = acum.T
    row = lax.broadcasted_iota(jnp.int32, (q, q), 0)
    col = lax.broadcasted_iota(jnp.int32, (q, q), 1)
    causal = col <= row

    groups = range(SSM_GROUPS)
    gw = HEADS_PER_GROUP * HEAD_DIM
    bm_b = [bm[:, g * SSM_STATE:(g + 1) * SSM_STATE].astype(BF16) for g in groups]
    cm_b = [cm[:, g * SSM_STATE:(g + 1) * SSM_STATE].astype(BF16) for g in groups]
    scores = [_dot_nt(cm_b[g], bm_b[g]) for g in groups]
    y_off = [_dot_nt(cm_b[g], s_ref[g * gw:(g + 1) * gw, :].astype(BF16)) for g in groups]
    y_parts, xdt_end, state_decay = [], [], []
    for h in range(HEADS):
        g, hl = divmod(h, HEADS_PER_GROUP)
        hs = slice(h * HEAD_DIM, (h + 1) * HEAD_DIM)
        a_col = acum[:, h:h + 1]
        a_row = acum_t[h:h + 1, :]
        a_last = acum_t[h:h + 1, q - 1:q]
        decay = jnp.where(causal, jnp.exp(a_col - a_row), 0.0)
        xdt = xs[:, hs] * dt[:, h:h + 1]
        y_h = _dot((scores[g] * decay).astype(BF16), xdt.astype(BF16))
        y_parts.append(y_h + y_off[g][:, hl * HEAD_DIM:(hl + 1) * HEAD_DIM] * jnp.exp(a_col))
        xdt_end.append(xdt * jnp.exp(a_last - a_col))
        state_decay.append(jnp.exp(a_last))
    for g in groups:
        xdt_g = jnp.concatenate(xdt_end[g * HEADS_PER_GROUP:(g + 1) * HEADS_PER_GROUP], axis=1)
        new = _dot(xdt_g.T.astype(BF16), bm_b[g])
        for hl in range(HEADS_PER_GROUP):
            h = g * HEADS_PER_GROUP + hl
            hs = slice(h * HEAD_DIM, (h + 1) * HEAD_DIM)
            s_ref[hs, :] = s_ref[hs, :] * state_decay[h] + new[hl * HEAD_DIM:(hl + 1) * HEAD_DIM, :]
    y = jnp.concatenate(y_parts, axis=1)

    y = y + xs * dskip_ref[...]
    y = y * (z_ref[...] * _sigmoid(z_ref[...]))
    y = y * lax.rsqrt(jnp.mean(y * y, axis=1, keepdims=True) + NORM_EPS) * g_ref[...]
    y_ref[...] = y.astype(BF16)

    @pl.when(c == nc - 1)
    def _():
        state_ref[0] = s_ref[...].reshape(HEADS, HEAD_DIM, SSM_STATE)


def _row128(v):
    return jnp.pad(v.astype(F32), (0, LANES - v.shape[0]))[None, :]


def _ssd_prompt(xbc, dt_raw, z, conv_w, conv_b, dt_bias, a_log, d_skip, g_ssm, bsz, seq):
    q = SSD_CHUNK
    nc = seq // q
    j = jnp.arange(q)
    tri = (j[None, :] <= j[:, None]).astype(BF16)
    row = lambda b, c: (b * nc + c, 0)
    const = lambda b, c: (0, 0)
    return pl.pallas_call(
        _ssd_body,
        grid=(bsz, nc),
        in_specs=[pl.BlockSpec((q, CONV_DIM), row),
                  pl.BlockSpec((q, LANES), row),
                  pl.BlockSpec((q, SSM_W), row),
                  pl.BlockSpec((CONV_W, CONV_DIM), const),
                  pl.BlockSpec((1, CONV_DIM), const),
                  pl.BlockSpec((1, LANES), const),
                  pl.BlockSpec((1, LANES), const),
                  pl.BlockSpec((1, SSM_W), const),
                  pl.BlockSpec((1, SSM_W), const),
                  pl.BlockSpec((q, q), const)],
        out_specs=[pl.BlockSpec((q, SSM_W), row),
                   pl.BlockSpec((1, HEADS, HEAD_DIM, SSM_STATE), lambda b, c: (b, 0, 0, 0)),
                   pl.BlockSpec((1, CONV_W - 1, CONV_DIM), lambda b, c: (b, 0, 0))],
        out_shape=[jax.ShapeDtypeStruct((bsz * seq, SSM_W), BF16),
                   jax.ShapeDtypeStruct((bsz, HEADS, HEAD_DIM, SSM_STATE), F32),
                   jax.ShapeDtypeStruct((bsz, CONV_W - 1, CONV_DIM), F32)],
        scratch_shapes=[pltpu.VMEM((q + SUBLANES, CONV_DIM), F32),
                        pltpu.VMEM((SSM_W, SSM_STATE), F32)],
        compiler_params=_cparams(("arbitrary", "arbitrary")),
        name="ssd_prompt",
    )(xbc, dt_raw, z, conv_w, conv_b[None, :], _row128(dt_bias), _row128(a_log),
      jnp.repeat(d_skip.astype(F32), HEAD_DIM)[None, :], g_ssm[None, :], tri)


def _ssd_dec_pre_body(xbc_ref, sc_ref, dt_ref, cw_ref, cb_ref, dtb_ref, alog_ref,
                      xs_ref, xdt_ref, bm_ref, cm_ref, dec_ref, cnew_ref):
    raw = xbc_ref[...]
    conv = raw * cw_ref[CONV_W - 1:CONV_W, :] + cb_ref[...]
    for i in range(CONV_W - 1):
        conv = conv + sc_ref[i] * cw_ref[i:i + 1, :]
    for i in range(CONV_W - 2):
        cnew_ref[i] = sc_ref[i + 1]
    cnew_ref[CONV_W - 2] = raw
    xc = conv * _sigmoid(conv)
    xs = xc[:, :SSM_W]
    xs_ref[...] = xs
    bm_ref[...] = xc[:, SSM_W:SSM_W + SSM_GROUPS * SSM_STATE]
    cm_ref[...] = xc[:, SSM_W + SSM_GROUPS * SSM_STATE:]
    dt = _softplus(dt_ref[...] + dtb_ref[...])
    dec_ref[...] = jnp.exp(dt * (-jnp.exp(alog_ref[...])))
    for h in range(HEADS):
        hs = slice(h * HEAD_DIM, (h + 1) * HEAD_DIM)
        xdt_ref[:, hs] = xs[:, hs] * dt[:, h:h + 1]


def _ssd_dec_pre(xbc, state_conv_l, dt_raw, conv_w, conv_b, dt_bias, a_log):
    nb = xbc.shape[0]
    gn = SSM_GROUPS * SSM_STATE
    return pl.pallas_call(
        _ssd_dec_pre_body,
        out_shape=[jax.ShapeDtypeStruct((nb, SSM_W), F32),
                   jax.ShapeDtypeStruct((nb, SSM_W), F32),
                   jax.ShapeDtypeStruct((nb, gn), F32),
                   jax.ShapeDtypeStruct((nb, gn), F32),
                   jax.ShapeDtypeStruct((nb, LANES), F32),
                   jax.ShapeDtypeStruct((CONV_W - 1, nb, CONV_DIM), F32)],
        compiler_params=pltpu.CompilerParams(vmem_limit_bytes=VMEM_LIMIT),
        name="ssd_decode_pre",
    )(xbc, state_conv_l, dt_raw, conv_w, conv_b[None, :], _row128(dt_bias), _row128(a_log))


SEQ_PER_STEP = 8


def _ssd_dec_state_body(dec_ref, xdt_ref, bm_ref, cm_ref, s_ref, snew_in_ref, snew_ref, y_ref):
    del snew_in_ref
    step = pl.program_id(0)
    seqs = range(SEQ_PER_STEP)
    xb = [jnp.broadcast_to(xdt_ref[0, :, i:i + 1], (SSM_W, SSM_STATE)) for i in seqs]
    hn = {}
    for i in seqs:
        b = step * SEQ_PER_STEP + i
        for h in range(HEADS):
            g = h // HEADS_PER_GROUP
            hs = slice(h * HEAD_DIM, (h + 1) * HEAD_DIM)
            brow = bm_ref[0, i:i + 1, g * SSM_STATE:(g + 1) * SSM_STATE]
            hn[i, h] = s_ref[i, h] * dec_ref[b, h] + xb[i][hs, :] * brow
            snew_ref[i, h] = hn[i, h]
    ys = {}
    for i in seqs:
        for h in range(HEADS):
            g = h // HEADS_PER_GROUP
            crow = cm_ref[0, i:i + 1, g * SSM_STATE:(g + 1) * SSM_STATE]
            ys[i, h] = jnp.sum(hn[i, h] * crow, axis=1, keepdims=True)
    for i in seqs:
        y_ref[0, :, i:i + 1] = jnp.concatenate([ys[i, h] for h in range(HEADS)], axis=0)


def _ssd_dec_state(layer, dec, xdt_t, bm3, cm3, state_all, new_all):
    nb = state_all.shape[1]
    steps = nb // SEQ_PER_STEP
    gn = SSM_GROUPS * SSM_STATE
    blk3 = lambda s: (s, 0, 0)
    blk5 = lambda s: (layer, s, 0, 0, 0)
    state_blk = pl.BlockSpec((None, SEQ_PER_STEP, HEADS, HEAD_DIM, SSM_STATE), blk5)
    return pl.pallas_call(
        _ssd_dec_state_body,
        grid=(steps,),
        in_specs=[pl.BlockSpec(memory_space=pltpu.SMEM),
                  pl.BlockSpec((1, SSM_W, SEQ_PER_STEP), blk3),
                  pl.BlockSpec((1, SEQ_PER_STEP, gn), blk3),
                  pl.BlockSpec((1, SEQ_PER_STEP, gn), blk3),
                  state_blk,
                  pl.BlockSpec(memory_space=pl.ANY)],
        out_specs=[state_blk, pl.BlockSpec((1, SSM_W, SEQ_PER_STEP), blk3)],
        out_shape=[jax.ShapeDtypeStruct(new_all.shape, F32),
                   jax.ShapeDtypeStruct((steps, SSM_W, SEQ_PER_STEP), F32)],
        input_output_aliases={5: 0},
        compiler_params=_cparams(("arbitrary",)),
        name="ssd_decode_state",
    )(dec, xdt_t, bm3, cm3, state_all, new_all)


def _gate_norm_body(y_ref, xs_ref, z_ref, dskip_ref, g_ref, o_ref):
    y = y_ref[...] + xs_ref[...] * dskip_ref[...]
    z = z_ref[...]
    y = y * (z * _sigmoid(z))
    y = y * lax.rsqrt(jnp.mean(y * y, axis=1, keepdims=True) + NORM_EPS) * g_ref[...]
    o_ref[...] = y.astype(BF16)


def _gate_norm(y, xs, z, d_skip, g_ssm):
    return pl.pallas_call(
        _gate_norm_body,
        out_shape=jax.ShapeDtypeStruct(y.shape, BF16),
        name="ssd_decode_gate",
    )(y, xs, z, jnp.repeat(d_skip.astype(F32), HEAD_DIM)[None, :], g_ssm[None, :])


def _layer_norm(u, g, b):
    mu = jnp.mean(u, axis=1, keepdims=True)
    d = u - mu
    var = jnp.mean(d * d, axis=1, keepdims=True)
    return d * lax.rsqrt(var + NORM_EPS) * g + b


def _outproj_body(att_ref, yg_ref, h_ref, wo_ref, gatt_ref, g1_ref, b1_ref, wr_ref, br_ref,
                  h1_ref, lg_ref):
    att = att_ref[...]
    att = att * lax.rsqrt(jnp.mean(att * att, axis=1, keepdims=True) + NORM_EPS) * gatt_ref[...]
    mix = _dot(att.astype(BF16), wo_ref[0:ATT_W, :]) + _dot(yg_ref[...], wo_ref[ATT_W:, :])
    h1 = _layer_norm(DN_ALPHA * h_ref[...] + mix, g1_ref[...], b1_ref[...])
    h1_ref[...] = h1
    lg_ref[...] = _dot_nt(h1.astype(BF16), wr_ref[...]) + br_ref[...]


def _outproj(att, yg, h, w_out_b, g_att, g1, b1, w_router_p, b_router_p, tm):
    t = att.shape[0]
    row = lambda i: (i, 0)
    const = lambda i: (0, 0)
    in_specs = [pl.BlockSpec((tm, ATT_W), row),
                pl.BlockSpec((tm, SSM_W), row),
                pl.BlockSpec((tm, D_MODEL), row),
                pl.BlockSpec((D_MODEL, D_MODEL), const),
                pl.BlockSpec((1, ATT_W), const),
                pl.BlockSpec((1, D_MODEL), const),
                pl.BlockSpec((1, D_MODEL), const),
                pl.BlockSpec((LANES, D_MODEL), const),
                pl.BlockSpec((1, LANES), const)]
    args = [att, yg, h, w_out_b, g_att[None, :], g1[None, :], b1[None, :], w_router_p, b_router_p]
    return pl.pallas_call(
        _outproj_body,
        grid=(t // tm,),
        in_specs=in_specs,
        out_specs=[pl.BlockSpec((tm, D_MODEL), row), pl.BlockSpec((tm, LANES), row)],
        out_shape=[jax.ShapeDtypeStruct((t, D_MODEL), F32),
                   jax.ShapeDtypeStruct((t, LANES), F32)],
        compiler_params=_cparams(("arbitrary",)),
        name="outproj",
    )(*args)


def _row_permute_body(tm, gather, idx_ref, *refs):
    if gather:
        src_ref, blk_ref, sem = refs
    else:
        blk_ref, _, dst_ref, sem = refs
    base = pl.program_id(0) * tm

    def copy(r):
        j = idx_ref[base + r]
        if gather:
            return pltpu.make_async_copy(src_ref.at[pl.ds(j, 1)], blk_ref.at[pl.ds(r, 1)], sem)
        return pltpu.make_async_copy(blk_ref.at[pl.ds(r, 1)], dst_ref.at[pl.ds(j, 1)], sem)

    def start(r, carry):
        copy(r).start()
        return carry

    def wait(r, carry):
        copy(r).wait()
        return carry

    lax.fori_loop(0, tm, start, 0, unroll=8)
    lax.fori_loop(0, tm, wait, 0, unroll=8)


def _row_permute(idx, src, dst, gather, tm):
    n_rows, d = idx.shape[0], src.shape[1]
    any_spec = pl.BlockSpec(memory_space=pl.ANY)
    blk_spec = pl.BlockSpec((tm, d), lambda i, idx: (i, 0))
    if gather:
        args, in_specs, out_specs, aliases = (idx, src), [any_spec], blk_spec, {}
        out_shape = jax.ShapeDtypeStruct((n_rows, d), src.dtype)
    else:
        args, in_specs, out_specs, aliases = (idx, src, dst), [blk_spec, any_spec], any_spec, {2: 0}
        out_shape = jax.ShapeDtypeStruct(dst.shape, dst.dtype)
    return pl.pallas_call(
        functools.partial(_row_permute_body, tm, gather),
        grid_spec=pltpu.PrefetchScalarGridSpec(
            num_scalar_prefetch=1,
            grid=(n_rows // tm,),
            in_specs=in_specs,
            out_specs=out_specs,
            scratch_shapes=[pltpu.SemaphoreType.DMA]),
        out_shape=out_shape,
        input_output_aliases=aliases,
        compiler_params=_cparams(("arbitrary",)),
        name="row_gather" if gather else "row_scatter",
    )(*args)


def _moe_body(lo_ref, hi_ref, first_ref, nact_ref,
              x_ref, wr_ref, br_ref, g2_ref, b2_ref,
              wg_lo, wu_lo, wd_lo, wg_hi, wu_hi, wd_hi,
              o_ref, wgb, wub, wdb):
    i = pl.program_id(0)

    @pl.when(i < nact_ref[0])
    def _():
        @pl.when(first_ref[i] == 1)
        def _():
            wgb[0] = wg_lo[...].astype(BF16)
            wub[0] = wu_lo[...].astype(BF16)
            wdb[0] = wd_lo[...].astype(BF16)
            wgb[1] = wg_hi[...].astype(BF16)
            wub[1] = wu_hi[...].astype(BF16)
            wdb[1] = wd_hi[...].astype(BF16)

        x = x_ref[...]
        xb = x.astype(BF16)
        logits = _dot_nt(xb, wr_ref[...]) + br_ref[...]
        lane = lax.broadcasted_iota(jnp.int32, logits.shape, 1)
        valid = lane < N_EXPERTS
        m = jnp.max(jnp.where(valid, logits, -jnp.inf), axis=1, keepdims=True)
        e = jnp.where(valid, jnp.exp(logits - m), 0.0)
        probs = e / jnp.sum(e, axis=1, keepdims=True)
        p_lo = jnp.sum(jnp.where(lane == lo_ref[i], probs, 0.0), axis=1, keepdims=True)
        p_hi = jnp.sum(jnp.where(lane == hi_ref[i], probs, 0.0), axis=1, keepdims=True)
        denom = p_lo + p_hi
        out = jnp.zeros_like(x)
        for slot, gate in ((0, p_lo / denom), (1, p_hi / denom)):
            hg = _dot(xb, wgb[slot])
            hu = _dot(xb, wub[slot])
            hmid = (hg * _sigmoid(hg)) * hu
            out = out + gate * _dot(hmid.astype(BF16), wdb[slot])
        o_ref[...] = _layer_norm(DN_ALPHA * x + out, g2_ref[...], b2_ref[...])

    @pl.when(i >= nact_ref[0])
    def _():
        o_ref[...] = jnp.zeros_like(o_ref)


def _moe_sorted(layer, x_sorted, tile_lo, tile_hi, tile_first, n_active,
                w_router_p, b_router_p, g2, b2, w_gate, w_up, w_down):
    n_tiles = tile_lo.shape[0]
    tm = TM_MOE

    def xmap(i, lo, hi, first, nact):
        return (jnp.minimum(i, nact[0] - 1), 0)

    const = lambda i, lo, hi, first, nact: (0, 0)
    w_lo = lambda i, lo, hi, first, nact: (layer, lo[i], 0, 0)
    w_hi = lambda i, lo, hi, first, nact: (layer, hi[i], 0, 0)
    up_blk = (None, None, D_MODEL, D_FF)
    dn_blk = (None, None, D_FF, D_MODEL)
    return pl.pallas_call(
        _moe_body,
        grid_spec=pltpu.PrefetchScalarGridSpec(
            num_scalar_prefetch=4,
            grid=(n_tiles,),
            in_specs=[pl.BlockSpec((tm, D_MODEL), xmap),
                      pl.BlockSpec((LANES, D_MODEL), const),
                      pl.BlockSpec((1, LANES), const),
                      pl.BlockSpec((1, D_MODEL), const),
                      pl.BlockSpec((1, D_MODEL), const),
                      pl.BlockSpec(up_blk, w_lo), pl.BlockSpec(up_blk, w_lo), pl.BlockSpec(dn_blk, w_lo),
                      pl.BlockSpec(up_blk, w_hi), pl.BlockSpec(up_blk, w_hi), pl.BlockSpec(dn_blk, w_hi)],
            out_specs=pl.BlockSpec((tm, D_MODEL), lambda i, lo, hi, first, nact: (i, 0)),
            scratch_shapes=[pltpu.VMEM((2, D_MODEL, D_FF), BF16),
                            pltpu.VMEM((2, D_MODEL, D_FF), BF16),
                            pltpu.VMEM((2, D_FF, D_MODEL), BF16)]),
        out_shape=jax.ShapeDtypeStruct(x_sorted.shape, F32),
        compiler_params=_cparams(("arbitrary",)),
        name="moe",
    )(tile_lo, tile_hi, tile_first, n_active,
      x_sorted, w_router_p, b_router_p, g2[None, :], b2[None, :],
      w_gate, w_up, w_down, w_gate, w_up, w_down)


def _route(logits, n_tiles):
    t = logits.shape[0]
    probs = jax.nn.softmax(logits, axis=-1)
    grouped = probs.reshape(t, N_EXPERT_GROUPS, EXPERTS_PER_GROUP)
    g_sel = jnp.argmax(grouped.max(-1), axis=-1)
    in_group = jnp.take_along_axis(grouped, g_sel[:, None, None], axis=1)[:, 0, :]
    _, top_i = lax.top_k(in_group, 2)
    e_lo = jnp.min(top_i, axis=-1)
    e_hi = jnp.max(top_i, axis=-1)
    pair = e_lo * (2 * EXPERTS_PER_GROUP - 1 - e_lo) // 2 + (e_hi - e_lo - 1)
    bucket = (g_sel * N_PAIRS + pair).astype(jnp.int32)

    onehot = (bucket[:, None] == jnp.arange(N_BUCKETS, dtype=jnp.int32)[None, :]).astype(jnp.int32)
    rank = jnp.sum((jnp.cumsum(onehot, axis=0) - onehot) * onehot, axis=1)
    counts = jnp.sum(onehot, axis=0)
    tiles_b = (counts + TM_MOE - 1) // TM_MOE
    tile_end = jnp.cumsum(tiles_b)
    tile_start = tile_end - tiles_b
    pos = (tile_start[bucket] * TM_MOE + rank).astype(jnp.int32)
    n_active = tile_end[-1]

    tile_id = jnp.minimum(jnp.arange(n_tiles, dtype=jnp.int32), n_active - 1)
    tile_bucket = jnp.sum((tile_id[:, None] >= tile_end[None, :]).astype(jnp.int32), axis=1)
    grp = tile_bucket // N_PAIRS
    pr = tile_bucket % N_PAIRS
    pair_lo = jnp.array([0, 0, 0, 1, 1, 2], jnp.int32)
    pair_hi = jnp.array([1, 2, 3, 2, 3, 3], jnp.int32)
    tile_lo = (grp * EXPERTS_PER_GROUP + pair_lo[pr]).astype(jnp.int32)
    tile_hi = (grp * EXPERTS_PER_GROUP + pair_hi[pr]).astype(jnp.int32)
    prev = jnp.concatenate([jnp.full((1,), -1, jnp.int32), tile_bucket[:-1].astype(jnp.int32)])
    tile_first = (tile_bucket != prev).astype(jnp.int32)
    return pos, tile_lo, tile_hi, tile_first, n_active.astype(jnp.int32)[None]


def kernel(x_prompt, x_sample, cache_k, cache_v, state_ssm, state_conv, page_table, w_in, att_bias,
           conv_w, conv_b, dt_bias, a_log, d_skip, attn_norm_g, ssm_norm_g, w_out, ln1_g, ln1_b,
           ln2_g, ln2_b, w_router, b_router, w_gate, w_up, w_down):
    bsz, seq, _ = x_prompt.shape
    nb = x_sample.shape[0]
    tp = bsz * seq
    t_all = tp + nb
    n_tiles = t_all // TM_MOE + N_BUCKETS
    pool, page = cache_k.shape[1], cache_k.shape[2]

    hp = x_prompt.reshape(tp, D_MODEL)
    hs = x_sample.reshape(nb, D_MODEL)
    cache_k4 = cache_k.transpose(0, 1, 3, 4, 2).reshape(DEPTH, pool, ATT_W, page)
    cache_v4 = cache_v.transpose(0, 1, 3, 4, 2).reshape(DEPTH, pool, ATT_W, page)
    state_conv_t = state_conv.transpose(0, 2, 1, 3)
    w_in_t = w_in.transpose(0, 2, 1).astype(BF16)
    w_main_t = w_in_t[:, :IN_MAIN, :]
    w_dt_t = jnp.pad(w_in_t[:, IN_MAIN:, :], ((0, 0), (0, LANES - HEADS), (0, 0)))
    w_out_b = w_out.astype(BF16)
    w_router_p = jnp.pad(w_router.T.astype(BF16), ((0, LANES - N_EXPERTS), (0, 0)))
    b_router_p = jnp.pad(b_router.astype(F32), (0, LANES - N_EXPERTS))[None, :]
    head_of_row = jnp.arange(2 * HEADS)[:, None]
    head_of_lane = jnp.arange(ATT_W)[None, :] // HEAD_DIM

    k_p = jnp.zeros((DEPTH, bsz, ATT_W, seq), F32)
    v_p = jnp.zeros((DEPTH, bsz, ATT_W, seq), F32)
    k_s = jnp.zeros((DEPTH, 1, ATT_W, nb), F32)
    v_s = jnp.zeros((DEPTH, 1, ATT_W, nb), F32)

    ssm_s = jnp.zeros(state_ssm.shape, F32)

    outs = [[] for _ in range(3)]
    for l in range(DEPTH):
        q8, kb, vb, k_p, v_p, z_p, xbc_p, dt_p = _inproj(
            l, hp, w_main_t[l], w_dt_t[l], k_p, v_p, bsz, TM_PROJ)
        q8s, _, _, k_s, v_s, z_s, xbc_s, dt_s = _inproj(
            l, hs, w_main_t[l], w_dt_t[l], k_s, v_s, 1, nb)
        q_rows = q8s.transpose(1, 0, 2).reshape(nb, 1, ATT_W)
        q_bd = jnp.where(head_of_row[None] == head_of_lane[None], q_rows, 0).astype(BF16)
        bias_b = jnp.broadcast_to(
            jnp.pad(att_bias[l].astype(F32), (0, HEADS))[:, None], (2 * HEADS, page))
        att_p, att_s = _attention(l, q8, kb, vb, att_bias[l].astype(F32), bsz, seq,
                                  q_bd, bias_b, cache_k4, cache_v4, page_table)
        att_s = att_s.reshape(nb, ATT_W)
        yg_p, ssm_p, conv_p = _ssd_prompt(xbc_p, dt_p, z_p, conv_w[l], conv_b[l], dt_bias[l], a_log[l],
                                          d_skip[l], ssm_norm_g[l], bsz, seq)
        xs_s, xdt_s, bm_s, cm_s, dec_s, conv_s = _ssd_dec_pre(
            xbc_s, state_conv_t[l], dt_s, conv_w[l], conv_b[l], dt_bias[l], a_log[l])
        steps = nb // SEQ_PER_STEP
        xdt_t = xdt_s.reshape(steps, SEQ_PER_STEP, SSM_W).transpose(0, 2, 1)
        ssm_s, y_t = _ssd_dec_state(l, dec_s, xdt_t,
                                    bm_s.reshape(steps, SEQ_PER_STEP, -1),
                                    cm_s.reshape(steps, SEQ_PER_STEP, -1), state_ssm, ssm_s)
        y_s = y_t.transpose(0, 2, 1).reshape(nb, SSM_W)
        yg_s = _gate_norm(y_s, xs_s, z_s, d_skip[l], ssm_norm_g[l])
        h1_p, lg_p = _outproj(att_p, yg_p, hp, w_out_b[l], attn_norm_g[l], ln1_g[l], ln1_b[l],
                              w_router_p, b_router_p, TM_PROJ)
        h1_s, lg_s = _outproj(att_s, yg_s, hs, w_out_b[l], attn_norm_g[l], ln1_g[l], ln1_b[l],
                              w_router_p, b_router_p, nb)
        logits = jnp.concatenate([lg_p[:, :N_EXPERTS], lg_s[:, :N_EXPERTS]], axis=0)
        pos, tile_lo, tile_hi, tile_first, n_active = _route(logits, n_tiles)
        x_sorted = jnp.zeros((n_tiles * TM_MOE, D_MODEL), F32)
        x_sorted = _row_permute(pos[:tp], h1_p, x_sorted, False, TM_PERM)
        x_sorted = _row_permute(pos[tp:], h1_s, x_sorted, False, nb)
        y_sorted = _moe_sorted(l, x_sorted, tile_lo, tile_hi, tile_first, n_active,
                               w_router_p, b_router_p, ln2_g[l], ln2_b[l], w_gate, w_up, w_down)
        hp = _row_permute(pos[:tp], y_sorted, None, True, TM_PERM)
        hs = _row_permute(pos[tp:], y_sorted, None, True, nb)

        for lst, val in zip(outs, (ssm_p, conv_p, conv_s)):
            lst.append(val)

    ssm_p, conv_p, conv_s = [jnp.stack(o) for o in outs]

    def untranspose(kt, n, length):
        return kt.reshape(DEPTH, n, HEADS, HEAD_DIM, length).transpose(0, 1, 4, 2, 3)

    return (hp.reshape(bsz, seq, D_MODEL),
            hs.reshape(nb, 1, D_MODEL),
            untranspose(k_p, bsz, seq),
            untranspose(v_p, bsz, seq),
            ssm_p,
            conv_p,
            untranspose(k_s, 1, nb).transpose(0, 2, 1, 3, 4),
            untranspose(v_s, 1, nb).transpose(0, 2, 1, 3, 4),
            ssm_s,
            conv_s.transpose(0, 2, 1, 3))
```

```python
import functools
import math

import jax
import jax.numpy as jnp
from jax import lax
from jax.experimental import pallas as pl
from jax.experimental.pallas import tpu as pltpu

F32 = jnp.float32
BF16 = jnp.bfloat16

D_MODEL = 1024
DEPTH = 4
HEADS = 8
HEAD_DIM = 64
ATT_W = HEADS * HEAD_DIM
SSM_W = HEADS * HEAD_DIM
SSM_STATE = 128
SSM_GROUPS = 2
HEADS_PER_GROUP = HEADS // SSM_GROUPS
CONV_W = 4
CONV_DIM = SSM_W + 2 * SSM_GROUPS * SSM_STATE
IN_MAIN = 3 * ATT_W + SSM_W + CONV_DIM
N_EXPERTS = 16
N_EXPERT_GROUPS = 4
EXPERTS_PER_GROUP = 4
N_PAIRS = 6
N_BUCKETS = N_EXPERT_GROUPS * N_PAIRS
D_FF = D_MODEL // 2
DN_ALPHA = (2 * DEPTH) ** 0.25
NORM_EPS = 1e-5
QK_SCALE = HEAD_DIM ** -0.5

LANES = 128
SUBLANES = 8
MXU_DIM = 256
VMEM_LIMIT = 56 * 1024 * 1024

TM_PROJ = 512
TQ = MXU_DIM
TK = MXU_DIM
SSD_CHUNK = 128
TM_MOE = 256
TM_PERM = 2048


def _cparams(sem):
    return pltpu.CompilerParams(dimension_semantics=sem, vmem_limit_bytes=VMEM_LIMIT)


def _sigmoid(x):
    return 1.0 / (1.0 + jnp.exp(-x))


def _softplus(x):
    return jnp.maximum(x, 0.0) + jnp.log(1.0 + jnp.exp(-jnp.abs(x)))


def _split3(x):
    hi = x.astype(BF16)
    r = x - hi.astype(F32)
    mid = r.astype(BF16)
    lo = (r - mid.astype(F32)).astype(BF16)
    return hi, mid, lo


def _dot(a, b):
    return jnp.dot(a, b, preferred_element_type=F32)


def _dot_nt(a, b):
    return lax.dot_general(a, b, (((1,), (1,)), ((), ())), preferred_element_type=F32)


def _inproj_body(x_ref, w_ref, wdt_ref, kt_in_ref, vt_in_ref,
                 q_ref, ktb_ref, vtb_ref, kt_ref, vt_ref, z_ref, xbc_ref, dt_ref):
    del kt_in_ref, vt_in_ref
    x = x_ref[...].astype(BF16)

    def mm(lo, hi):
        return _dot_nt(x, w_ref[lo:hi, :])

    def mm_t(lo, hi):
        return _dot_nt(w_ref[lo:hi, :], x)

    q = mm(0, ATT_W) * QK_SCALE
    for h in range(HEADS):
        q_ref[h] = q[:, h * HEAD_DIM:(h + 1) * HEAD_DIM].astype(BF16)
    kt = mm_t(ATT_W, 2 * ATT_W)
    kt_ref[...] = kt
    ktb_ref[...] = kt.astype(BF16)
    vt = mm_t(2 * ATT_W, 3 * ATT_W)
    vt_ref[...] = vt
    vtb_ref[...] = vt.astype(BF16)
    z_ref[...] = mm(3 * ATT_W, 3 * ATT_W + SSM_W)
    xbc_ref[...] = mm(3 * ATT_W + SSM_W, IN_MAIN)
    dt_ref[...] = _dot_nt(x, wdt_ref[...])


def _inproj(layer, x, w_t, wdt_t, kt_all, vt_all, nseq, tm):
    t = x.shape[0]
    seq = t // nseq
    per_seq = seq // tm
    row = lambda i: (i, 0)
    head = lambda i: (0, i, 0)
    tr = lambda i: (i // per_seq, 0, i % per_seq)
    tr_all = lambda i: (layer, i // per_seq, 0, i % per_seq)
    const = lambda i: (0, 0)
    any_spec = pl.BlockSpec(memory_space=pl.ANY)
    return pl.pallas_call(
        _inproj_body,
        grid=(t // tm,),
        in_specs=[pl.BlockSpec((tm, D_MODEL), row),
                  pl.BlockSpec((IN_MAIN, D_MODEL), const),
                  pl.BlockSpec((LANES, D_MODEL), const),
                  any_spec, any_spec],
        out_specs=[pl.BlockSpec((HEADS, tm, HEAD_DIM), head),
                   pl.BlockSpec((None, ATT_W, tm), tr),
                   pl.BlockSpec((None, ATT_W, tm), tr),
                   pl.BlockSpec((None, None, ATT_W, tm), tr_all),
                   pl.BlockSpec((None, None, ATT_W, tm), tr_all),
                   pl.BlockSpec((tm, SSM_W), row),
                   pl.BlockSpec((tm, CONV_DIM), row),
                   pl.BlockSpec((tm, LANES), row)],
        out_shape=[jax.ShapeDtypeStruct((HEADS, t, HEAD_DIM), BF16),
                   jax.ShapeDtypeStruct((nseq, ATT_W, seq), BF16),
                   jax.ShapeDtypeStruct((nseq, ATT_W, seq), BF16),
                   jax.ShapeDtypeStruct(kt_all.shape, F32),
                   jax.ShapeDtypeStruct(vt_all.shape, F32),
                   jax.ShapeDtypeStruct((t, SSM_W), F32),
                   jax.ShapeDtypeStruct((t, CONV_DIM), F32),
                   jax.ShapeDtypeStruct((t, LANES), F32)],
        input_output_aliases={3: 3, 4: 4},
        compiler_params=_cparams(("arbitrary",)),
        name="inproj",
    )(x, w_t, wdt_t, kt_all, vt_all)


ATTN_SKEW = 1


def _attn_key_tile(bias_ref, q_ref, k_ref, v_ref, u, acc_ref, c_ref, start, mask):
    log_beta, l1m, p = {}, {}, {}
    for k in range(HEADS + 2 * ATTN_SKEW):
        h = k
        if h < HEADS:
            kt = k_ref[h * HEAD_DIM:(h + 1) * HEAD_DIM, pl.ds(start, TK)]
            z = _dot(q_ref[h], kt) + bias_ref[h]
            log_beta[h] = jnp.minimum(z, 0.0) - jnp.log(1.0 + jnp.exp(-jnp.abs(z)))
            l1m[h] = log_beta[h] - z
            if mask is not None:
                l1m[h] = jnp.where(mask, l1m[h], 0.0)
        h = k - ATTN_SKEW
        if 0 <= h < HEADS:
            suffix = _dot(l1m[h].astype(BF16), u)
            ph = jnp.exp(log_beta.pop(h) + suffix)
            if mask is not None:
                ph = jnp.where(mask, ph, 0.0)
            p[h] = ph.astype(BF16)
        h = k - 2 * ATTN_SKEW
        if 0 <= h < HEADS:
            vt = v_ref[h * HEAD_DIM:(h + 1) * HEAD_DIM, pl.ds(start, TK)]
            pv = _dot_nt(p.pop(h), vt)
            dc = jnp.sum(l1m.pop(h), axis=1, keepdims=True)
            if mask is not None:
                acc_ref[h] = pv
                c_ref[h] = dc
            else:
                c = c_ref[h]
                acc_ref[h] += jnp.exp(c) * pv
                c_ref[h] = c + dc


def _suffix_matrix(n):
    j = jnp.arange(n)
    return (j[:, None] > j[None, :]).astype(BF16)


def _decode_sequence(n_pages, q, bias, u, hm, kbuf, vbuf, slot):
    pages = range(n_pages)
    z = [_dot(q, kbuf[slot, p].astype(BF16)) + bias for p in pages]
    log_beta = [jnp.minimum(zp, 0.0) - jnp.log(1.0 + jnp.exp(-jnp.abs(zp))) for zp in z]
    l1m = [lb - zp for lb, zp in zip(log_beta, z)]
    parts = [jnp.concatenate(_split3(l), axis=0) for l in l1m]
    sums = [_dot(pt, u) for pt in parts]
    nh = 2 * HEADS
    suffix = [s[0:nh] + s[nh:2 * nh] + s[2 * nh:3 * nh] for s in sums]
    w = [jnp.exp(lb + sf).astype(BF16) for lb, sf in zip(log_beta, suffix)]
    pv = [_dot_nt(w[p], vbuf[slot, p].astype(BF16)) for p in pages]
    c = jnp.zeros((nh, 1), F32)
    acc = jnp.zeros((nh, ATT_W), F32)
    for p in reversed(pages):
        acc = acc + jnp.exp(c) * pv[p]
        c = c + jnp.sum(l1m[p], axis=1, keepdims=True)
    return jnp.sum(acc * hm, axis=0, keepdims=True)


SEQ_PER_ATTN_STEP = 2


def _attn_body(n_pages, layer, bias_ref, pt_ref,
               q_ref, k_ref, v_ref, u_ref, qd_ref, biasd_ref, ud_ref, hm_ref, k_hbm, v_hbm,
               o_ref, od_ref, acc_ref, c_ref, kbuf, vbuf, sem):
    qi = pl.program_id(1)
    step = pl.program_id(0) * pl.num_programs(1) + qi
    n_steps = pl.num_programs(0) * pl.num_programs(1)
    seq0 = step * SEQ_PER_ATTN_STEP

    def page_copies(seq_idx, s):
        copies = []
        for p in range(n_pages):
            pg = pt_ref[seq_idx, p]
            copies.append(pltpu.make_async_copy(k_hbm.at[layer, pg], kbuf.at[s, p], sem.at[0, s]))
            copies.append(pltpu.make_async_copy(v_hbm.at[layer, pg], vbuf.at[s, p], sem.at[1, s]))
        return copies

    def decode(j):
        for cp in page_copies(seq0 + j, j):
            cp.wait()
        od_ref[j] = _decode_sequence(n_pages, qd_ref[j], biasd_ref[...], ud_ref[...], hm_ref[...],
                                     kbuf, vbuf, j)

    @pl.when(step == 0)
    def _():
        for cp in page_copies(0, 0):
            cp.start()

    for cp in page_copies(seq0 + 1, 1):
        cp.start()
    decode(0)

    @pl.when(step + 1 < n_steps)
    def _():
        for cp in page_copies(seq0 + SEQ_PER_ATTN_STEP, 0):
            cp.start()

    u = u_ref[...]
    row = lax.broadcasted_iota(jnp.int32, (TQ, TK), 0)
    col = lax.broadcasted_iota(jnp.int32, (TQ, TK), 1)
    _attn_key_tile(bias_ref, q_ref, k_ref, v_ref, u, acc_ref, c_ref,
                   pl.multiple_of(qi * TK, TK), col < row)

    def key_step(jj, carry):
        _attn_key_tile(bias_ref, q_ref, k_ref, v_ref, u, acc_ref, c_ref,
                       pl.multiple_of((qi - 1 - jj) * TK, TK), None)
        return carry

    lax.fori_loop(0, qi, key_step, 0)
    for h in range(HEADS):
        o_ref[:, h * HEAD_DIM:(h + 1) * HEAD_DIM] = acc_ref[h]

    decode(1)


def _attention(layer, q8, kb, vb, bias, bsz, seq, q_bd, bias_b, cache_k4, cache_v4, page_table):
    nq = seq // TQ
    nb, n_pages = page_table.shape
    page = cache_k4.shape[3]
    assert nb == SEQ_PER_ATTN_STEP * bsz * nq
    hm = (jnp.arange(ATT_W)[None, :] // HEAD_DIM == jnp.arange(2 * HEADS)[:, None]).astype(F32)
    const = lambda b, i, bias, pt: (0, 0)
    per_step = lambda b, i, bias, pt: (b * nq + i, 0, 0)
    any_spec = pl.BlockSpec(memory_space=pl.ANY)
    return pl.pallas_call(
        functools.partial(_attn_body, n_pages, layer),
        grid_spec=pltpu.PrefetchScalarGridSpec(
            num_scalar_prefetch=2,
            grid=(bsz, nq),
            in_specs=[pl.BlockSpec((HEADS, TQ, HEAD_DIM), lambda b, i, bias, pt: (0, b * nq + i, 0)),
                      pl.BlockSpec((None, ATT_W, seq), lambda b, i, bias, pt: (b, 0, 0)),
                      pl.BlockSpec((None, ATT_W, seq), lambda b, i, bias, pt: (b, 0, 0)),
                      pl.BlockSpec((TK, TK), const),
                      pl.BlockSpec((SEQ_PER_ATTN_STEP, 2 * HEADS, ATT_W), per_step),
                      pl.BlockSpec((2 * HEADS, page), const),
                      pl.BlockSpec((page, page), const),
                      pl.BlockSpec((2 * HEADS, ATT_W), const),
                      any_spec, any_spec],
            out_specs=[pl.BlockSpec((TQ, ATT_W), lambda b, i, bias, pt: (b * nq + i, 0)),
                       pl.BlockSpec((SEQ_PER_ATTN_STEP, 1, ATT_W), per_step)],
            scratch_shapes=[pltpu.VMEM((HEADS, TQ, HEAD_DIM), F32),
                            pltpu.VMEM((HEADS, TQ, 1), F32),
                            pltpu.VMEM((SEQ_PER_ATTN_STEP, n_pages, ATT_W, page), F32),
                            pltpu.VMEM((SEQ_PER_ATTN_STEP, n_pages, ATT_W, page), F32),
                            pltpu.SemaphoreType.DMA((2, SEQ_PER_ATTN_STEP))]),
        out_shape=[jax.ShapeDtypeStruct((bsz * seq, ATT_W), F32),
                   jax.ShapeDtypeStruct((nb, 1, ATT_W), F32)],
        compiler_params=_cparams(("arbitrary", "arbitrary")),
        name="attention",
    )(bias, page_table, q8, kb, vb, _suffix_matrix(TK),
      q_bd, bias_b, _suffix_matrix(page), hm, cache_k4, cache_v4)


SSD_SEQS = 2


def _ssd_body(xbc_ref, dt_ref, z_ref, cw_ref, cb_ref, dtb_ref, alog_ref, dskip_ref, g_ref, tri_ref,
              y_ref, state_ref, tail_ref, ext_ref, s_ref):
    c = pl.program_id(1)
    nc = pl.num_programs(1)
    q = SSD_CHUNK
    pad = SUBLANES
    seqs = range(SSD_SEQS)
    groups = range(SSM_GROUPS)
    gw = HEADS_PER_GROUP * HEAD_DIM
    n0 = SSM_W
    n1 = SSM_W + SSM_GROUPS * SSM_STATE

    @pl.when(c == 0)
    def _():
        for s in seqs:
            ext_ref[s, 0:pad, :] = jnp.zeros((pad, CONV_DIM), F32)
        s_ref[...] = jnp.zeros_like(s_ref)

    @pl.when(c > 0)
    def _():
        for s in seqs:
            ext_ref[s, 0:pad, :] = ext_ref[s, q:q + pad, :]

    raw = [xbc_ref[s] for s in seqs]
    for s in seqs:
        ext_ref[s, pad:pad + q, :] = raw[s]

    @pl.when(c == nc - 1)
    def _():
        for s in seqs:
            tail_ref[s] = raw[s][q - (CONV_W - 1):, :]

    xc = []
    for s in seqs:
        conv = raw[s] * cw_ref[CONV_W - 1:CONV_W, :] + cb_ref[...]
        for i in range(1, CONV_W):
            conv = conv + ext_ref[s, pad - i:pad - i + q, :] * cw_ref[CONV_W - 1 - i:CONV_W - i, :]
        xc.append(conv * _sigmoid(conv))
    xs = [x[:, :SSM_W] for x in xc]
    bm_b = [[x[:, n0 + g * SSM_STATE:n0 + (g + 1) * SSM_STATE].astype(BF16) for g in groups] for x in xc]
    cm_b = [[x[:, n1 + g * SSM_STATE:n1 + (g + 1) * SSM_STATE].astype(BF16) for g in groups] for x in xc]

    a = -jnp.exp(alog_ref[...])
    tri = tri_ref[...]
    dt = [_softplus(dt_ref[s] + dtb_ref[...]) for s in seqs]
    da3 = [_split3(d * a) for d in dt]
    acum = [_dot(tri, p[0]) + _dot(tri, p[1]) + _dot(tri, p[2]) for p in da3]
    acum_t = [ac.T for ac in acum]
    row = lax.broadcasted_iota(jnp.int32, (q, q), 0)
    col = lax.broadcasted_iota(jnp.int32, (q, q), 1)
    causal = col <= row

    scores, y_off = {}, {}
    for g in groups:
        for s in seqs:
            scores[s, g] = _dot_nt(cm_b[s][g], bm_b[s][g])
            y_off[s, g] = _dot_nt(cm_b[s][g], s_ref[s, g * gw:(g + 1) * gw, :].astype(BF16))
    y_parts = [[] for _ in seqs]
    xdt_end = [[] for _ in seqs]
    state_decay = [[] for _ in seqs]
    for h in range(HEADS):
        g, hl = divmod(h, HEADS_PER_GROUP)
        hs = slice(h * HEAD_DIM, (h + 1) * HEAD_DIM)
        for s in seqs:
            a_col = acum[s][:, h:h + 1]
            a_row = acum_t[s][h:h + 1, :]
            a_last = acum_t[s][h:h + 1, q - 1:q]
            decay = jnp.where(causal, jnp.exp(a_col - a_row), 0.0)
            xdt = xs[s][:, hs] * dt[s][:, h:h + 1]
            y_h = _dot((scores[s, g] * decay).astype(BF16), xdt.astype(BF16))
            y_parts[s].append(y_h + y_off[s, g][:, hl * HEAD_DIM:(hl + 1) * HEAD_DIM] * jnp.exp(a_col))
            xdt_end[s].append(xdt * jnp.exp(a_last - a_col))
            state_decay[s].append(jnp.exp(a_last))
    for g in groups:
        for s in seqs:
            xdt_g = jnp.concatenate(xdt_end[s][g * HEADS_PER_GROUP:(g + 1) * HEADS_PER_GROUP], axis=1)
            new = _dot(xdt_g.T.astype(BF16), bm_b[s][g])
            for hl in range(HEADS_PER_GROUP):
                h = g * HEADS_PER_GROUP + hl
                hs = slice(h * HEAD_DIM, (h + 1) * HEAD_DIM)
                s_ref[s, hs, :] = (s_ref[s, hs, :] * state_decay[s][h]
                                   + new[hl * HEAD_DIM:(hl + 1) * HEAD_DIM, :])
    for s in seqs:
        y = jnp.concatenate(y_parts[s], axis=1) + xs[s] * dskip_ref[...]
        z = z_ref[s]
        y = y * (z * _sigmoid(z))
        y = y * lax.rsqrt(jnp.mean(y * y, axis=1, keepdims=True) + NORM_EPS) * g_ref[...]
        y_ref[s] = y.astype(BF16)

    @pl.when(c == nc - 1)
    def _():
        for s in seqs:
            state_ref[s] = s_ref[s].reshape(HEADS, HEAD_DIM, SSM_STATE)


def _row128(v):
    return jnp.pad(v.astype(F32), (0, LANES - v.shape[0]))[None, :]


def _ssd_prompt(xbc, dt_raw, z, conv_w, conv_b, dt_bias, a_log, d_skip, g_ssm, bsz, seq):
    q = SSD_CHUNK
    nc = seq // q
    j = jnp.arange(q)
    tri = (j[None, :] <= j[:, None]).astype(BF16)
    ns = SSD_SEQS
    chunk = lambda b, c: (b, c, 0)
    const = lambda b, c: (0, 0)
    yg, state, tail = pl.pallas_call(
        _ssd_body,
        grid=(bsz // ns, nc),
        in_specs=[pl.BlockSpec((ns, q, CONV_DIM), chunk),
                  pl.BlockSpec((ns, q, LANES), chunk),
                  pl.BlockSpec((ns, q, SSM_W), chunk),
                  pl.BlockSpec((CONV_W, CONV_DIM), const),
                  pl.BlockSpec((1, CONV_DIM), const),
                  pl.BlockSpec((1, LANES), const),
                  pl.BlockSpec((1, LANES), const),
                  pl.BlockSpec((1, SSM_W), const),
                  pl.BlockSpec((1, SSM_W), const),
                  pl.BlockSpec((q, q), const)],
        out_specs=[pl.BlockSpec((ns, q, SSM_W), chunk),
                   pl.BlockSpec((ns, HEADS, HEAD_DIM, SSM_STATE), lambda b, c: (b, 0, 0, 0)),
                   pl.BlockSpec((ns, CONV_W - 1, CONV_DIM), lambda b, c: (b, 0, 0))],
        out_shape=[jax.ShapeDtypeStruct((bsz, seq, SSM_W), BF16),
                   jax.ShapeDtypeStruct((bsz, HEADS, HEAD_DIM, SSM_STATE), F32),
                   jax.ShapeDtypeStruct((bsz, CONV_W - 1, CONV_DIM), F32)],
        scratch_shapes=[pltpu.VMEM((ns, q + SUBLANES, CONV_DIM), F32),
                        pltpu.VMEM((ns, SSM_W, SSM_STATE), F32)],
        compiler_params=_cparams(("arbitrary", "arbitrary")),
        name="ssd_prompt",
    )(xbc.reshape(bsz, seq, CONV_DIM), dt_raw.reshape(bsz, seq, LANES), z.reshape(bsz, seq, SSM_W),
      conv_w, conv_b[None, :], _row128(dt_bias), _row128(a_log),
      jnp.repeat(d_skip.astype(F32), HEAD_DIM)[None, :], g_ssm[None, :], tri)
    return yg.reshape(bsz * seq, SSM_W), state, tail


def _ssd_dec_pre_body(xbc_ref, sc_ref, dt_ref, cw_ref, cb_ref, dtb_ref, alog_ref,
                      xs_ref, xdt_ref, bm_ref, cm_ref, dec_ref, cnew_ref):
    raw = xbc_ref[...]
    conv = raw * cw_ref[CONV_W - 1:CONV_W, :] + cb_ref[...]
    for i in range(CONV_W - 1):
        conv = conv + sc_ref[i] * cw_ref[i:i + 1, :]
    for i in range(CONV_W - 2):
        cnew_ref[i] = sc_ref[i + 1]
    cnew_ref[CONV_W - 2] = raw
    xc = conv * _sigmoid(conv)
    xs = xc[:, :SSM_W]
    xs_ref[...] = xs
    bm_ref[...] = xc[:, SSM_W:SSM_W + SSM_GROUPS * SSM_STATE]
    cm_ref[...] = xc[:, SSM_W + SSM_GROUPS * SSM_STATE:]
    dt = _softplus(dt_ref[...] + dtb_ref[...])
    dec_ref[...] = jnp.exp(dt * (-jnp.exp(alog_ref[...])))
    for h in range(HEADS):
        hs = slice(h * HEAD_DIM, (h + 1) * HEAD_DIM)
        xdt_ref[:, hs] = xs[:, hs] * dt[:, h:h + 1]


def _ssd_dec_pre(xbc, state_conv_l, dt_raw, conv_w, conv_b, dt_bias, a_log):
    nb = xbc.shape[0]
    gn = SSM_GROUPS * SSM_STATE
    return pl.pallas_call(
        _ssd_dec_pre_body,
        out_shape=[jax.ShapeDtypeStruct((nb, SSM_W), F32),
                   jax.ShapeDtypeStruct((nb, SSM_W), F32),
                   jax.ShapeDtypeStruct((nb, gn), F32),
                   jax.ShapeDtypeStruct((nb, gn), F32),
                   jax.ShapeDtypeStruct((nb, LANES), F32),
                   jax.ShapeDtypeStruct((CONV_W - 1, nb, CONV_DIM), F32)],
        compiler_params=pltpu.CompilerParams(vmem_limit_bytes=VMEM_LIMIT),
        name="ssd_decode_pre",
    )(xbc, state_conv_l, dt_raw, conv_w, conv_b[None, :], _row128(dt_bias), _row128(a_log))


SEQ_PER_STEP = 8


def _ssd_dec_state_body(dec_ref, xdt_ref, bm_ref, cm_ref, s_ref, snew_in_ref, snew_ref, y_ref):
    del snew_in_ref
    step = pl.program_id(0)
    seqs = range(SEQ_PER_STEP)
    xb = [jnp.broadcast_to(xdt_ref[0, :, i:i + 1], (SSM_W, SSM_STATE)) for i in seqs]
    hn = {}
    for i in seqs:
        b = step * SEQ_PER_STEP + i
        for h in range(HEADS):
            g = h // HEADS_PER_GROUP
            hs = slice(h * HEAD_DIM, (h + 1) * HEAD_DIM)
            brow = bm_ref[0, i:i + 1, g * SSM_STATE:(g + 1) * SSM_STATE]
            hn[i, h] = s_ref[i, h] * dec_ref[b, h] + xb[i][hs, :] * brow
            snew_ref[i, h] = hn[i, h]
    ys = {}
    for i in seqs:
        for h in range(HEADS):
            g = h // HEADS_PER_GROUP
            crow = cm_ref[0, i:i + 1, g * SSM_STATE:(g + 1) * SSM_STATE]
            ys[i, h] = jnp.sum(hn[i, h] * crow, axis=1, keepdims=True)
    for i in seqs:
        y_ref[0, :, i:i + 1] = jnp.concatenate([ys[i, h] for h in range(HEADS)], axis=0)


def _ssd_dec_state(layer, dec, xdt_t, bm3, cm3, state_all, new_all):
    nb = state_all.shape[1]
    steps = nb // SEQ_PER_STEP
    gn = SSM_GROUPS * SSM_STATE
    blk3 = lambda s: (s, 0, 0)
    blk5 = lambda s: (layer, s, 0, 0, 0)
    state_blk = pl.BlockSpec((None, SEQ_PER_STEP, HEADS, HEAD_DIM, SSM_STATE), blk5)
    return pl.pallas_call(
        _ssd_dec_state_body,
        grid=(steps,),
        in_specs=[pl.BlockSpec(memory_space=pltpu.SMEM),
                  pl.BlockSpec((1, SSM_W, SEQ_PER_STEP), blk3),
                  pl.BlockSpec((1, SEQ_PER_STEP, gn), blk3),
                  pl.BlockSpec((1, SEQ_PER_STEP, gn), blk3),
                  state_blk,
                  pl.BlockSpec(memory_space=pl.ANY)],
        out_specs=[state_blk, pl.BlockSpec((1, SSM_W, SEQ_PER_STEP), blk3)],
        out_shape=[jax.ShapeDtypeStruct(new_all.shape, F32),
                   jax.ShapeDtypeStruct((steps, SSM_W, SEQ_PER_STEP), F32)],
        input_output_aliases={5: 0},
        compiler_params=_cparams(("arbitrary",)),
        name="ssd_decode_state",
    )(dec, xdt_t, bm3, cm3, state_all, new_all)


def _gate_norm_body(y_ref, xs_ref, z_ref, dskip_ref, g_ref, o_ref):
    y = y_ref[...] + xs_ref[...] * dskip_ref[...]
    z = z_ref[...]
    y = y * (z * _sigmoid(z))
    y = y * lax.rsqrt(jnp.mean(y * y, axis=1, keepdims=True) + NORM_EPS) * g_ref[...]
    o_ref[...] = y.astype(BF16)


def _gate_norm(y, xs, z, d_skip, g_ssm):
    return pl.pallas_call(
        _gate_norm_body,
        out_shape=jax.ShapeDtypeStruct(y.shape, BF16),
        name="ssd_decode_gate",
    )(y, xs, z, jnp.repeat(d_skip.astype(F32), HEAD_DIM)[None, :], g_ssm[None, :])


def _layer_norm(u, g, b):
    mu = jnp.mean(u, axis=1, keepdims=True)
    d = u - mu
    var = jnp.mean(d * d, axis=1, keepdims=True)
    return d * lax.rsqrt(var + NORM_EPS) * g + b


def _outproj_body(att_ref, yg_ref, h_ref, wo_ref, gatt_ref, g1_ref, b1_ref, wr_ref, br_ref,
                  h1_ref, lg_ref):
    att = att_ref[...]
    att = att * lax.rsqrt(jnp.mean(att * att, axis=1, keepdims=True) + NORM_EPS) * gatt_ref[...]
    mix = _dot(att.astype(BF16), wo_ref[0:ATT_W, :]) + _dot(yg_ref[...], wo_ref[ATT_W:, :])
    h1 = _layer_norm(DN_ALPHA * h_ref[...] + mix, g1_ref[...], b1_ref[...])
    h1_ref[...] = h1
    lg_ref[...] = _dot_nt(h1.astype(BF16), wr_ref[...]) + br_ref[...]


def _outproj(att, yg, h, w_out_b, g_att, g1, b1, w_router_p, b_router_p, tm):
    t = att.shape[0]
    row = lambda i: (i, 0)
    const = lambda i: (0, 0)
    in_specs = [pl.BlockSpec((tm, ATT_W), row),
                pl.BlockSpec((tm, SSM_W), row),
                pl.BlockSpec((tm, D_MODEL), row),
                pl.BlockSpec((D_MODEL, D_MODEL), const),
                pl.BlockSpec((1, ATT_W), const),
                pl.BlockSpec((1, D_MODEL), const),
                pl.BlockSpec((1, D_MODEL), const),
                pl.BlockSpec((LANES, D_MODEL), const),
                pl.BlockSpec((1, LANES), const)]
    args = [att, yg, h, w_out_b, g_att[None, :], g1[None, :], b1[None, :], w_router_p, b_router_p]
    return pl.pallas_call(
        _outproj_body,
        grid=(t // tm,),
        in_specs=in_specs,
        out_specs=[pl.BlockSpec((tm, D_MODEL), row), pl.BlockSpec((tm, LANES), row)],
        out_shape=[jax.ShapeDtypeStruct((t, D_MODEL), F32),
                   jax.ShapeDtypeStruct((t, LANES), F32)],
        compiler_params=_cparams(("arbitrary",)),
        name="outproj",
    )(*args)


def _row_permute_body(tm, gather, idx_ref, *refs):
    if gather:
        src_ref, blk_ref, sem = refs
    else:
        blk_ref, _, dst_ref, sem = refs
    base = pl.program_id(0) * tm

    def copy(r):
        j = idx_ref[base + r]
        if gather:
            return pltpu.make_async_copy(src_ref.at[pl.ds(j, 1)], blk_ref.at[pl.ds(r, 1)], sem)
        return pltpu.make_async_copy(blk_ref.at[pl.ds(r, 1)], dst_ref.at[pl.ds(j, 1)], sem)

    def start(r, carry):
        copy(r).start()
        return carry

    def wait(r, carry):
        copy(r).wait()
        return carry

    lax.fori_loop(0, tm, start, 0, unroll=8)
    lax.fori_loop(0, tm, wait, 0, unroll=8)


def _row_permute(idx, src, dst, gather, tm):
    n_rows, d = idx.shape[0], src.shape[1]
    any_spec = pl.BlockSpec(memory_space=pl.ANY)
    blk_spec = pl.BlockSpec((tm, d), lambda i, idx: (i, 0))
    if gather:
        args, in_specs, out_specs, aliases = (idx, src), [any_spec], blk_spec, {}
        out_shape = jax.ShapeDtypeStruct((n_rows, d), src.dtype)
    else:
        args, in_specs, out_specs, aliases = (idx, src, dst), [blk_spec, any_spec], any_spec, {2: 0}
        out_shape = jax.ShapeDtypeStruct(dst.shape, dst.dtype)
    return pl.pallas_call(
        functools.partial(_row_permute_body, tm, gather),
        grid_spec=pltpu.PrefetchScalarGridSpec(
            num_scalar_prefetch=1,
            grid=(n_rows // tm,),
            in_specs=in_specs,
            out_specs=out_specs,
            scratch_shapes=[pltpu.SemaphoreType.DMA]),
        out_shape=out_shape,
        input_output_aliases=aliases,
        compiler_params=_cparams(("arbitrary",)),
        name="row_gather" if gather else "row_scatter",
    )(*args)


def _moe_body(lo_ref, hi_ref, first_ref, nact_ref,
              x_ref, wr_ref, br_ref, g2_ref, b2_ref,
              wg_lo, wu_lo, wd_lo, wg_hi, wu_hi, wd_hi,
              o_ref, wgb, wub, wdb):
    i = pl.program_id(0)

    @pl.when(i < nact_ref[0])
    def _():
        @pl.when(first_ref[i] == 1)
        def _():
            wgb[0] = wg_lo[...].astype(BF16)
            wub[0] = wu_lo[...].astype(BF16)
            wdb[0] = wd_lo[...].astype(BF16)
            wgb[1] = wg_hi[...].astype(BF16)
            wub[1] = wu_hi[...].astype(BF16)
            wdb[1] = wd_hi[...].astype(BF16)

        x = x_ref[...]
        xb = x.astype(BF16)
        logits = _dot_nt(xb, wr_ref[...]) + br_ref[...]
        lane = lax.broadcasted_iota(jnp.int32, logits.shape, 1)
        valid = lane < N_EXPERTS
        m = jnp.max(jnp.where(valid, logits, -jnp.inf), axis=1, keepdims=True)
        e = jnp.where(valid, jnp.exp(logits - m), 0.0)
        probs = e / jnp.sum(e, axis=1, keepdims=True)
        p_lo = jnp.sum(jnp.where(lane == lo_ref[i], probs, 0.0), axis=1, keepdims=True)
        p_hi = jnp.sum(jnp.where(lane == hi_ref[i], probs, 0.0), axis=1, keepdims=True)
        denom = p_lo + p_hi
        out = jnp.zeros_like(x)
        for slot, gate in ((0, p_lo / denom), (1, p_hi / denom)):
            hg = _dot(xb, wgb[slot])
            hu = _dot(xb, wub[slot])
            hmid = (hg * _sigmoid(hg)) * hu
            out = out + gate * _dot(hmid.astype(BF16), wdb[slot])
        o_ref[...] = _layer_norm(DN_ALPHA * x + out, g2_ref[...], b2_ref[...])

    @pl.when(i >= nact_ref[0])
    def _():
        o_ref[...] = jnp.zeros_like(o_ref)


def _moe_sorted(layer, x_sorted, tile_lo, tile_hi, tile_first, n_active,
                w_router_p, b_router_p, g2, b2, w_gate, w_up, w_down):
    n_tiles = tile_lo.shape[0]
    tm = TM_MOE

    def xmap(i, lo, hi, first, nact):
        return (jnp.minimum(i, nact[0] - 1), 0)

    const = lambda i, lo, hi, first, nact: (0, 0)
    w_lo = lambda i, lo, hi, first, nact: (layer, lo[i], 0, 0)
    w_hi = lambda i, lo, hi, first, nact: (layer, hi[i], 0, 0)
    up_blk = (None, None, D_MODEL, D_FF)
    dn_blk = (None, None, D_FF, D_MODEL)
    return pl.pallas_call(
        _moe_body,
        grid_spec=pltpu.PrefetchScalarGridSpec(
            num_scalar_prefetch=4,
            grid=(n_tiles,),
            in_specs=[pl.BlockSpec((tm, D_MODEL), xmap),
                      pl.BlockSpec((LANES, D_MODEL), const),
                      pl.BlockSpec((1, LANES), const),
                      pl.BlockSpec((1, D_MODEL), const),
                      pl.BlockSpec((1, D_MODEL), const),
                      pl.BlockSpec(up_blk, w_lo), pl.BlockSpec(up_blk, w_lo), pl.BlockSpec(dn_blk, w_lo),
                      pl.BlockSpec(up_blk, w_hi), pl.BlockSpec(up_blk, w_hi), pl.BlockSpec(dn_blk, w_hi)],
            out_specs=pl.BlockSpec((tm, D_MODEL), lambda i, lo, hi, first, nact: (i, 0)),
            scratch_shapes=[pltpu.VMEM((2, D_MODEL, D_FF), BF16),
                            pltpu.VMEM((2, D_MODEL, D_FF), BF16),
                            pltpu.VMEM((2, D_FF, D_MODEL), BF16)]),
        out_shape=jax.ShapeDtypeStruct(x_sorted.shape, F32),
        compiler_params=_cparams(("arbitrary",)),
        name="moe",
    )(tile_lo, tile_hi, tile_first, n_active,
      x_sorted, w_router_p, b_router_p, g2[None, :], b2[None, :],
      w_gate, w_up, w_down, w_gate, w_up, w_down)


def _route(logits, n_tiles):
    t = logits.shape[0]
    probs = jax.nn.softmax(logits, axis=-1)
    grouped = probs.reshape(t, N_EXPERT_GROUPS, EXPERTS_PER_GROUP)
    g_sel = jnp.argmax(grouped.max(-1), axis=-1)
    g_hot = (jnp.arange(N_EXPERT_GROUPS)[None, :] == g_sel[:, None])
    in_group = jnp.sum(jnp.where(g_hot[:, :, None], grouped, 0.0), axis=1)
    lane = jnp.arange(EXPERTS_PER_GROUP)[None, :]
    i1 = jnp.argmax(in_group, axis=-1)
    i2 = jnp.argmax(jnp.where(lane == i1[:, None], -jnp.inf, in_group), axis=-1)
    e_lo = jnp.minimum(i1, i2)
    e_hi = jnp.maximum(i1, i2)
    pair = e_lo * (2 * EXPERTS_PER_GROUP - 1 - e_lo) // 2 + (e_hi - e_lo - 1)
    bucket = (g_sel * N_PAIRS + pair).astype(jnp.int32)

    onehot = (bucket[:, None] == jnp.arange(N_BUCKETS, dtype=jnp.int32)[None, :]).astype(jnp.int32)
    rank = jnp.sum((jnp.cumsum(onehot, axis=0) - onehot) * onehot, axis=1)
    counts = jnp.sum(onehot, axis=0)
    tiles_b = (counts + TM_MOE - 1) // TM_MOE
    tile_end = jnp.cumsum(tiles_b)
    tile_start = tile_end - tiles_b
    pos = (jnp.sum(onehot * tile_start[None, :], axis=1) * TM_MOE + rank).astype(jnp.int32)
    n_active = tile_end[-1]

    tile_id = jnp.minimum(jnp.arange(n_tiles, dtype=jnp.int32), n_active - 1)
    tile_bucket = jnp.sum((tile_id[:, None] >= tile_end[None, :]).astype(jnp.int32), axis=1)
    grp = tile_bucket // N_PAIRS
    pr = tile_bucket % N_PAIRS
    pair_lo = (pr >= 3).astype(jnp.int32) + (pr >= 5).astype(jnp.int32)
    pair_hi = jnp.where(pair_lo == 0, pr + 1, jnp.where(pair_lo == 1, pr - 1, 3))
    tile_lo = (grp * EXPERTS_PER_GROUP + pair_lo).astype(jnp.int32)
    tile_hi = (grp * EXPERTS_PER_GROUP + pair_hi).astype(jnp.int32)
    prev = jnp.concatenate([jnp.full((1,), -1, jnp.int32), tile_bucket[:-1].astype(jnp.int32)])
    tile_first = (tile_bucket != prev).astype(jnp.int32)
    return pos, tile_lo, tile_hi, tile_first, n_active.astype(jnp.int32)[None]


def kernel(x_prompt, x_sample, cache_k, cache_v, state_ssm, state_conv, page_table, w_in, att_bias,
           conv_w, conv_b, dt_bias, a_log, d_skip, attn_norm_g, ssm_norm_g, w_out, ln1_g, ln1_b,
           ln2_g, ln2_b, w_router, b_router, w_gate, w_up, w_down):
    bsz, seq, _ = x_prompt.shape
    nb = x_sample.shape[0]
    tp = bsz * seq
    t_all = tp + nb
    n_tiles = t_all // TM_MOE + N_BUCKETS
    pool, page = cache_k.shape[1], cache_k.shape[2]

    hp = x_prompt.reshape(tp, D_MODEL)
    hs = x_sample.reshape(nb, D_MODEL)
    cache_k4 = cache_k.transpose(0, 1, 3, 4, 2).reshape(DEPTH, pool, ATT_W, page)
    cache_v4 = cache_v.transpose(0, 1, 3, 4, 2).reshape(DEPTH, pool, ATT_W, page)
    state_conv_t = state_conv.transpose(0, 2, 1, 3)
    w_in_t = w_in.transpose(0, 2, 1).astype(BF16)
    w_main_t = w_in_t[:, :IN_MAIN, :]
    w_dt_t = jnp.pad(w_in_t[:, IN_MAIN:, :], ((0, 0), (0, LANES - HEADS), (0, 0)))
    w_out_b = w_out.astype(BF16)
    w_router_p = jnp.pad(w_router.T.astype(BF16), ((0, LANES - N_EXPERTS), (0, 0)))
    b_router_p = jnp.pad(b_router.astype(F32), (0, LANES - N_EXPERTS))[None, :]
    head_of_row = jnp.arange(2 * HEADS)[:, None]
    head_of_lane = jnp.arange(ATT_W)[None, :] // HEAD_DIM

    k_p = jnp.zeros((DEPTH, bsz, ATT_W, seq), F32)
    v_p = jnp.zeros((DEPTH, bsz, ATT_W, seq), F32)
    k_s = jnp.zeros((DEPTH, 1, ATT_W, nb), F32)
    v_s = jnp.zeros((DEPTH, 1, ATT_W, nb), F32)

    ssm_s = jnp.zeros(state_ssm.shape, F32)

    outs = [[] for _ in range(3)]
    for l in range(DEPTH):
        q8, kb, vb, k_p, v_p, z_p, xbc_p, dt_p = _inproj(
            l, hp, w_main_t[l], w_dt_t[l], k_p, v_p, bsz, TM_PROJ)
        q8s, _, _, k_s, v_s, z_s, xbc_s, dt_s = _inproj(
            l, hs, w_main_t[l], w_dt_t[l], k_s, v_s, 1, nb)
        q_rows = q8s.transpose(1, 0, 2).reshape(nb, 1, ATT_W)
        q_bd = jnp.where(head_of_row[None] == head_of_lane[None], q_rows, 0).astype(BF16)
        bias_b = jnp.broadcast_to(
            jnp.pad(att_bias[l].astype(F32), (0, HEADS))[:, None], (2 * HEADS, page))
        att_p, att_s = _attention(l, q8, kb, vb, att_bias[l].astype(F32), bsz, seq,
                                  q_bd, bias_b, cache_k4, cache_v4, page_table)
        att_s = att_s.reshape(nb, ATT_W)
        yg_p, ssm_p, conv_p = _ssd_prompt(xbc_p, dt_p, z_p, conv_w[l], conv_b[l], dt_bias[l], a_log[l],
                                          d_skip[l], ssm_norm_g[l], bsz, seq)
        xs_s, xdt_s, bm_s, cm_s, dec_s, conv_s = _ssd_dec_pre(
            xbc_s, state_conv_t[l], dt_s, conv_w[l], conv_b[l], dt_bias[l], a_log[l])
        steps = nb // SEQ_PER_STEP
        xdt_t = xdt_s.reshape(steps, SEQ_PER_STEP, SSM_W).transpose(0, 2, 1)
        ssm_s, y_t = _ssd_dec_state(l, dec_s, xdt_t,
                                    bm_s.reshape(steps, SEQ_PER_STEP, -1),
                                    cm_s.reshape(steps, SEQ_PER_STEP, -1), state_ssm, ssm_s)
        y_s = y_t.transpose(0, 2, 1).reshape(nb, SSM_W)
        yg_s = _gate_norm(y_s, xs_s, z_s, d_skip[l], ssm_norm_g[l])
        h1_p, lg_p = _outproj(att_p, yg_p, hp, w_out_b[l], attn_norm_g[l], ln1_g[l], ln1_b[l],
                              w_router_p, b_router_p, TM_PROJ)
        h1_s, lg_s = _outproj(att_s, yg_s, hs, w_out_b[l], attn_norm_g[l], ln1_g[l], ln1_b[l],
                              w_router_p, b_router_p, nb)
        logits = jnp.concatenate([lg_p[:, :N_EXPERTS], lg_s[:, :N_EXPERTS]], axis=0)
        pos, tile_lo, tile_hi, tile_first, n_active = _route(logits, n_tiles)
        x_sorted = jnp.zeros((n_tiles * TM_MOE, D_MODEL), F32)
        x_sorted = _row_permute(pos[:tp], h1_p, x_sorted, False, min(TM_PERM, tp))
        x_sorted = _row_permute(pos[tp:], h1_s, x_sorted, False, nb)
        y_sorted = _moe_sorted(l, x_sorted, tile_lo, tile_hi, tile_first, n_active,
                               w_router_p, b_router_p, ln2_g[l], ln2_b[l], w_gate, w_up, w_down)
        hp = _row_permute(pos[:tp], y_sorted, None, True, min(TM_PERM, tp))
        hs = _row_permute(pos[tp:], y_sorted, None, True, nb)

        for lst, val in zip(outs, (ssm_p, conv_p, conv_s)):
            lst.append(val)

    ssm_p, conv_p, conv_s = [jnp.stack(o) for o in outs]

    def untranspose(kt, n, length):
        return kt.reshape(DEPTH, n, HEADS, HEAD_DIM, length).transpose(0, 1, 4, 2, 3)

    return (hp.reshape(bsz, seq, D_MODEL),
            hs.reshape(nb, 1, D_MODEL),
            untranspose(k_p, bsz, seq),
            untranspose(v_p, bsz, seq),
            ssm_p,
            conv_p,
            untranspose(k_s, 1, nb).transpose(0, 2, 1, 3, 4),
            untranspose(v_s, 1, nb).transpose(0, 2, 1, 3, 4),
            ssm_s,
            conv_s.transpose(0, 2, 1, 3))
```

```python
import functools
import math

import jax
import jax.numpy as jnp
from jax import lax
from jax.experimental import pallas as pl
from jax.experimental.pallas import tpu as pltpu

F32 = jnp.float32
BF16 = jnp.bfloat16

D_MODEL = 1024
DEPTH = 4
HEADS = 8
HEAD_DIM = 64
ATT_W = HEADS * HEAD_DIM
SSM_W = HEADS * HEAD_DIM
SSM_STATE = 128
SSM_GROUPS = 2
HEADS_PER_GROUP = HEADS // SSM_GROUPS
CONV_W = 4
CONV_DIM = SSM_W + 2 * SSM_GROUPS * SSM_STATE
IN_MAIN = 3 * ATT_W + SSM_W + CONV_DIM
N_EXPERTS = 16
N_EXPERT_GROUPS = 4
EXPERTS_PER_GROUP = 4
N_PAIRS = 6
N_BUCKETS = N_EXPERT_GROUPS * N_PAIRS
D_FF = D_MODEL // 2
DN_ALPHA = (2 * DEPTH) ** 0.25
NORM_EPS = 1e-5
QK_SCALE = HEAD_DIM ** -0.5

LANES = 128
SUBLANES = 8
MXU_DIM = 256
VMEM_LIMIT = 56 * 1024 * 1024

TM_PROJ = 512
TQ = MXU_DIM
TK = MXU_DIM
SSD_CHUNK = 128
TM_MOE = 256
TM_PERM = 2048


def _cparams(sem):
    return pltpu.CompilerParams(dimension_semantics=sem, vmem_limit_bytes=VMEM_LIMIT)


def _sigmoid(x):
    return 1.0 / (1.0 + jnp.exp(-x))


def _softplus(x):
    return jnp.maximum(x, 0.0) + jnp.log(1.0 + jnp.exp(-jnp.abs(x)))


def _split3(x):
    hi = x.astype(BF16)
    r = x - hi.astype(F32)
    mid = r.astype(BF16)
    lo = (r - mid.astype(F32)).astype(BF16)
    return hi, mid, lo


def _dot(a, b):
    return jnp.dot(a, b, preferred_element_type=F32)


def _dot_nt(a, b):
    return lax.dot_general(a, b, (((1,), (1,)), ((), ())), preferred_element_type=F32)


def _inproj_body(x_ref, w_ref, wdt_ref, kt_in_ref, vt_in_ref,
                 q_ref, ktb_ref, vtb_ref, kt_ref, vt_ref, z_ref, xbc_ref, dt_ref):
    del kt_in_ref, vt_in_ref
    x = x_ref[...].astype(BF16)

    def mm(lo, hi):
        return _dot_nt(x, w_ref[lo:hi, :])

    def mm_t(lo, hi):
        return _dot_nt(w_ref[lo:hi, :], x)

    q = mm(0, ATT_W) * QK_SCALE
    for h in range(HEADS):
        q_ref[h] = q[:, h * HEAD_DIM:(h + 1) * HEAD_DIM].astype(BF16)
    kt = mm_t(ATT_W, 2 * ATT_W)
    kt_ref[...] = kt
    ktb_ref[...] = kt.astype(BF16)
    vt = mm_t(2 * ATT_W, 3 * ATT_W)
    vt_ref[...] = vt
    vtb_ref[...] = vt.astype(BF16)
    z_ref[...] = mm(3 * ATT_W, 3 * ATT_W + SSM_W)
    xbc_ref[...] = mm(3 * ATT_W + SSM_W, IN_MAIN)
    dt_ref[...] = _dot_nt(x, wdt_ref[...])


def _inproj(layer, x, w_t, wdt_t, kt_all, vt_all, nseq, tm):
    t = x.shape[0]
    seq = t // nseq
    per_seq = seq // tm
    row = lambda i: (i, 0)
    head = lambda i: (0, i, 0)
    tr = lambda i: (i // per_seq, 0, i % per_seq)
    tr_all = lambda i: (layer, i // per_seq, 0, i % per_seq)
    const = lambda i: (0, 0)
    any_spec = pl.BlockSpec(memory_space=pl.ANY)
    return pl.pallas_call(
        _inproj_body,
        grid=(t // tm,),
        in_specs=[pl.BlockSpec((tm, D_MODEL), row),
                  pl.BlockSpec((IN_MAIN, D_MODEL), const),
                  pl.BlockSpec((LANES, D_MODEL), const),
                  any_spec, any_spec],
        out_specs=[pl.BlockSpec((HEADS, tm, HEAD_DIM), head),
                   pl.BlockSpec((None, ATT_W, tm), tr),
                   pl.BlockSpec((None, ATT_W, tm), tr),
                   pl.BlockSpec((None, None, ATT_W, tm), tr_all),
                   pl.BlockSpec((None, None, ATT_W, tm), tr_all),
                   pl.BlockSpec((tm, SSM_W), row),
                   pl.BlockSpec((tm, CONV_DIM), row),
                   pl.BlockSpec((tm, LANES), row)],
        out_shape=[jax.ShapeDtypeStruct((HEADS, t, HEAD_DIM), BF16),
                   jax.ShapeDtypeStruct((nseq, ATT_W, seq), BF16),
                   jax.ShapeDtypeStruct((nseq, ATT_W, seq), BF16),
                   jax.ShapeDtypeStruct(kt_all.shape, F32),
                   jax.ShapeDtypeStruct(vt_all.shape, F32),
                   jax.ShapeDtypeStruct((t, SSM_W), F32),
                   jax.ShapeDtypeStruct((t, CONV_DIM), F32),
                   jax.ShapeDtypeStruct((t, LANES), F32)],
        input_output_aliases={3: 3, 4: 4},
        compiler_params=_cparams(("arbitrary",)),
        name="inproj",
    )(x, w_t, wdt_t, kt_all, vt_all)


ATTN_SKEW = 1


def _attn_key_tile(bias_ref, q_ref, k_ref, v_ref, u, acc_ref, c_ref, start, mask):
    log_beta, l1m, p = {}, {}, {}
    for k in range(HEADS + 2 * ATTN_SKEW):
        h = k
        if h < HEADS:
            kt = k_ref[h * HEAD_DIM:(h + 1) * HEAD_DIM, pl.ds(start, TK)]
            z = _dot(q_ref[h], kt) + bias_ref[h]
            log_beta[h] = jnp.minimum(z, 0.0) - jnp.log(1.0 + jnp.exp(-jnp.abs(z)))
            l1m[h] = log_beta[h] - z
            if mask is not None:
                l1m[h] = jnp.where(mask, l1m[h], 0.0)
        h = k - ATTN_SKEW
        if 0 <= h < HEADS:
            suffix = _dot(l1m[h].astype(BF16), u)
            ph = jnp.exp(log_beta.pop(h) + suffix)
            if mask is not None:
                ph = jnp.where(mask, ph, 0.0)
            p[h] = ph.astype(BF16)
        h = k - 2 * ATTN_SKEW
        if 0 <= h < HEADS:
            vt = v_ref[h * HEAD_DIM:(h + 1) * HEAD_DIM, pl.ds(start, TK)]
            pv = _dot_nt(p.pop(h), vt)
            dc = jnp.sum(l1m.pop(h), axis=1, keepdims=True)
            if mask is not None:
                acc_ref[h] = pv
                c_ref[h] = dc
            else:
                c = c_ref[h]
                acc_ref[h] += jnp.exp(c) * pv
                c_ref[h] = c + dc


def _suffix_matrix(n):
    j = jnp.arange(n)
    return (j[:, None] > j[None, :]).astype(BF16)


def _decode_sequence(n_pages, q, bias, u, hm, kbuf, vbuf, slot):
    pages = range(n_pages)
    z = [_dot(q, kbuf[slot, p].astype(BF16)) + bias for p in pages]
    log_beta = [jnp.minimum(zp, 0.0) - jnp.log(1.0 + jnp.exp(-jnp.abs(zp))) for zp in z]
    l1m = [lb - zp for lb, zp in zip(log_beta, z)]
    parts = [jnp.concatenate(_split3(l), axis=0) for l in l1m]
    sums = [_dot(pt, u) for pt in parts]
    nh = 2 * HEADS
    suffix = [s[0:nh] + s[nh:2 * nh] + s[2 * nh:3 * nh] for s in sums]
    w = [jnp.exp(lb + sf).astype(BF16) for lb, sf in zip(log_beta, suffix)]
    pv = [_dot_nt(w[p], vbuf[slot, p].astype(BF16)) for p in pages]
    c = jnp.zeros((nh, 1), F32)
    acc = jnp.zeros((nh, ATT_W), F32)
    for p in reversed(pages):
        acc = acc + jnp.exp(c) * pv[p]
        c = c + jnp.sum(l1m[p], axis=1, keepdims=True)
    return jnp.sum(acc * hm, axis=0, keepdims=True)


SEQ_PER_ATTN_STEP = 2


def _attn_body(n_pages, layer, bias_ref, pt_ref,
               q_ref, k_ref, v_ref, u_ref, qd_ref, biasd_ref, ud_ref, hm_ref, k_hbm, v_hbm,
               o_ref, od_ref, acc_ref, c_ref, kbuf, vbuf, sem):
    qi = pl.program_id(1)
    step = pl.program_id(0) * pl.num_programs(1) + qi
    n_steps = pl.num_programs(0) * pl.num_programs(1)
    seq0 = step * SEQ_PER_ATTN_STEP

    def page_copies(seq_idx, s):
        copies = []
        for p in range(n_pages):
            pg = pt_ref[seq_idx, p]
            copies.append(pltpu.make_async_copy(k_hbm.at[layer, pg], kbuf.at[s, p], sem.at[0, s]))
            copies.append(pltpu.make_async_copy(v_hbm.at[layer, pg], vbuf.at[s, p], sem.at[1, s]))
        return copies

    def decode(j):
        for cp in page_copies(seq0 + j, j):
            cp.wait()
        od_ref[j] = _decode_sequence(n_pages, qd_ref[j], biasd_ref[...], ud_ref[...], hm_ref[...],
                                     kbuf, vbuf, j)

    @pl.when(step == 0)
    def _():
        for cp in page_copies(0, 0):
            cp.start()

    for cp in page_copies(seq0 + 1, 1):
        cp.start()
    decode(0)

    @pl.when(step + 1 < n_steps)
    def _():
        for cp in page_copies(seq0 + SEQ_PER_ATTN_STEP, 0):
            cp.start()

    u = u_ref[...]
    row = lax.broadcasted_iota(jnp.int32, (TQ, TK), 0)
    col = lax.broadcasted_iota(jnp.int32, (TQ, TK), 1)
    _attn_key_tile(bias_ref, q_ref, k_ref, v_ref, u, acc_ref, c_ref,
                   pl.multiple_of(qi * TK, TK), col < row)

    def key_step(jj, carry):
        _attn_key_tile(bias_ref, q_ref, k_ref, v_ref, u, acc_ref, c_ref,
                       pl.multiple_of((qi - 1 - jj) * TK, TK), None)
        return carry

    lax.fori_loop(0, qi, key_step, 0)
    for h in range(HEADS):
        o_ref[:, h * HEAD_DIM:(h + 1) * HEAD_DIM] = acc_ref[h]

    decode(1)


def _attention(layer, q8, kb, vb, bias, bsz, seq, q_bd, bias_b, cache_k4, cache_v4, page_table):
    nq = seq // TQ
    nb, n_pages = page_table.shape
    page = cache_k4.shape[3]
    assert nb == SEQ_PER_ATTN_STEP * bsz * nq
    hm = (jnp.arange(ATT_W)[None, :] // HEAD_DIM == jnp.arange(2 * HEADS)[:, None]).astype(F32)
    const = lambda b, i, bias, pt: (0, 0)
    per_step = lambda b, i, bias, pt: (b * nq + i, 0, 0)
    any_spec = pl.BlockSpec(memory_space=pl.ANY)
    return pl.pallas_call(
        functools.partial(_attn_body, n_pages, layer),
        grid_spec=pltpu.PrefetchScalarGridSpec(
            num_scalar_prefetch=2,
            grid=(bsz, nq),
            in_specs=[pl.BlockSpec((HEADS, TQ, HEAD_DIM), lambda b, i, bias, pt: (0, b * nq + i, 0)),
                      pl.BlockSpec((None, ATT_W, seq), lambda b, i, bias, pt: (b, 0, 0)),
                      pl.BlockSpec((None, ATT_W, seq), lambda b, i, bias, pt: (b, 0, 0)),
                      pl.BlockSpec((TK, TK), const),
                      pl.BlockSpec((SEQ_PER_ATTN_STEP, 2 * HEADS, ATT_W), per_step),
                      pl.BlockSpec((2 * HEADS, page), const),
                      pl.BlockSpec((page, page), const),
                      pl.BlockSpec((2 * HEADS, ATT_W), const),
                      any_spec, any_spec],
            out_specs=[pl.BlockSpec((TQ, ATT_W), lambda b, i, bias, pt: (b * nq + i, 0)),
                       pl.BlockSpec((SEQ_PER_ATTN_STEP, 1, ATT_W), per_step)],
            scratch_shapes=[pltpu.VMEM((HEADS, TQ, HEAD_DIM), F32),
                            pltpu.VMEM((HEADS, TQ, 1), F32),
                            pltpu.VMEM((SEQ_PER_ATTN_STEP, n_pages, ATT_W, page), F32),
                            pltpu.VMEM((SEQ_PER_ATTN_STEP, n_pages, ATT_W, page), F32),
                            pltpu.SemaphoreType.DMA((2, SEQ_PER_ATTN_STEP))]),
        out_shape=[jax.ShapeDtypeStruct((bsz * seq, ATT_W), F32),
                   jax.ShapeDtypeStruct((nb, 1, ATT_W), F32)],
        compiler_params=_cparams(("arbitrary", "arbitrary")),
        name="attention",
    )(bias, page_table, q8, kb, vb, _suffix_matrix(TK),
      q_bd, bias_b, _suffix_matrix(page), hm, cache_k4, cache_v4)


SSD_SEQS = 2


def _ssd_body(xbc_ref, dt_ref, z_ref, cw_ref, cb_ref, dtb_ref, alog_ref, dskip_ref, g_ref, tri_ref,
              y_ref, state_ref, tail_ref, ext_ref, s_ref):
    c = pl.program_id(1)
    nc = pl.num_programs(1)
    q = SSD_CHUNK
    pad = SUBLANES
    seqs = range(SSD_SEQS)
    groups = range(SSM_GROUPS)
    gw = HEADS_PER_GROUP * HEAD_DIM
    n0 = SSM_W
    n1 = SSM_W + SSM_GROUPS * SSM_STATE

    @pl.when(c == 0)
    def _():
        for s in seqs:
            ext_ref[s, 0:pad, :] = jnp.zeros((pad, CONV_DIM), F32)
        s_ref[...] = jnp.zeros_like(s_ref)

    @pl.when(c > 0)
    def _():
        for s in seqs:
            ext_ref[s, 0:pad, :] = ext_ref[s, q:q + pad, :]

    raw = [xbc_ref[s] for s in seqs]
    for s in seqs:
        ext_ref[s, pad:pad + q, :] = raw[s]

    @pl.when(c == nc - 1)
    def _():
        for s in seqs:
            tail_ref[s] = raw[s][q - (CONV_W - 1):, :]

    xc = []
    for s in seqs:
        conv = raw[s] * cw_ref[CONV_W - 1:CONV_W, :] + cb_ref[...]
        for i in range(1, CONV_W):
            conv = conv + ext_ref[s, pad - i:pad - i + q, :] * cw_ref[CONV_W - 1 - i:CONV_W - i, :]
        xc.append(conv * _sigmoid(conv))
    xs = [x[:, :SSM_W] for x in xc]
    bm_b = [[x[:, n0 + g * SSM_STATE:n0 + (g + 1) * SSM_STATE].astype(BF16) for g in groups] for x in xc]
    cm_b = [[x[:, n1 + g * SSM_STATE:n1 + (g + 1) * SSM_STATE].astype(BF16) for g in groups] for x in xc]

    a = -jnp.exp(alog_ref[...])
    tri = tri_ref[...]
    dt = [_softplus(dt_ref[s] + dtb_ref[...]) for s in seqs]
    da3 = [_split3(d * a) for d in dt]
    acum = [_dot(tri, p[0]) + _dot(tri, p[1]) + _dot(tri, p[2]) for p in da3]
    acum_t = [ac.T for ac in acum]
    row = lax.broadcasted_iota(jnp.int32, (q, q), 0)
    col = lax.broadcasted_iota(jnp.int32, (q, q), 1)
    causal = col <= row

    scores, y_off = {}, {}
    for g in groups:
        for s in seqs:
            scores[s, g] = _dot_nt(cm_b[s][g], bm_b[s][g])
            y_off[s, g] = _dot_nt(cm_b[s][g], s_ref[s, g * gw:(g + 1) * gw, :].astype(BF16))
    y_parts = [[] for _ in seqs]
    xdt_end = [[] for _ in seqs]
    state_decay = [[] for _ in seqs]
    for h in range(HEADS):
        g, hl = divmod(h, HEADS_PER_GROUP)
        hs = slice(h * HEAD_DIM, (h + 1) * HEAD_DIM)
        for s in seqs:
            a_col = jnp.broadcast_to(acum[s][:, h:h + 1], (q, q))
            a_row = acum_t[s][h:h + 1, :]
            a_last = acum_t[s][h:h + 1, q - 1:q]
            decay = jnp.where(causal, jnp.exp(a_col - a_row), 0.0)
            a_col_p = a_col[:, :HEAD_DIM]
            xdt = xs[s][:, hs] * dt[s][:, h:h + 1]
            y_h = _dot((scores[s, g] * decay).astype(BF16), xdt.astype(BF16))
            y_parts[s].append(y_h + y_off[s, g][:, hl * HEAD_DIM:(hl + 1) * HEAD_DIM] * jnp.exp(a_col_p))
            xdt_end[s].append(xdt * jnp.exp(a_last - a_col_p))
            state_decay[s].append(jnp.exp(a_last))
    for g in groups:
        for s in seqs:
            xdt_g = jnp.concatenate(xdt_end[s][g * HEADS_PER_GROUP:(g + 1) * HEADS_PER_GROUP], axis=1)
            new = _dot(xdt_g.T.astype(BF16), bm_b[s][g])
            for hl in range(HEADS_PER_GROUP):
                h = g * HEADS_PER_GROUP + hl
                hs = slice(h * HEAD_DIM, (h + 1) * HEAD_DIM)
                s_ref[s, hs, :] = (s_ref[s, hs, :] * state_decay[s][h]
                                   + new[hl * HEAD_DIM:(hl + 1) * HEAD_DIM, :])
    for s in seqs:
        y = jnp.concatenate(y_parts[s], axis=1) + xs[s] * dskip_ref[...]
        z = z_ref[s]
        y = y * (z * _sigmoid(z))
        y = y * lax.rsqrt(jnp.mean(y * y, axis=1, keepdims=True) + NORM_EPS) * g_ref[...]
        y_ref[s] = y.astype(BF16)

    @pl.when(c == nc - 1)
    def _():
        for s in seqs:
            state_ref[s] = s_ref[s].reshape(HEADS, HEAD_DIM, SSM_STATE)


def _row128(v):
    return jnp.pad(v.astype(F32), (0, LANES - v.shape[0]))[None, :]


def _ssd_prompt(xbc, dt_raw, z, conv_w, conv_b, dt_bias, a_log, d_skip, g_ssm, bsz, seq):
    q = SSD_CHUNK
    nc = seq // q
    j = jnp.arange(q)
    tri = (j[None, :] <= j[:, None]).astype(BF16)
    ns = SSD_SEQS
    chunk = lambda b, c: (b, c, 0)
    const = lambda b, c: (0, 0)
    yg, state, tail = pl.pallas_call(
        _ssd_body,
        grid=(bsz // ns, nc),
        in_specs=[pl.BlockSpec((ns, q, CONV_DIM), chunk),
                  pl.BlockSpec((ns, q, LANES), chunk),
                  pl.BlockSpec((ns, q, SSM_W), chunk),
                  pl.BlockSpec((CONV_W, CONV_DIM), const),
                  pl.BlockSpec((1, CONV_DIM), const),
                  pl.BlockSpec((1, LANES), const),
                  pl.BlockSpec((1, LANES), const),
                  pl.BlockSpec((1, SSM_W), const),
                  pl.BlockSpec((1, SSM_W), const),
                  pl.BlockSpec((q, q), const)],
        out_specs=[pl.BlockSpec((ns, q, SSM_W), chunk),
                   pl.BlockSpec((ns, HEADS, HEAD_DIM, SSM_STATE), lambda b, c: (b, 0, 0, 0)),
                   pl.BlockSpec((ns, CONV_W - 1, CONV_DIM), lambda b, c: (b, 0, 0))],
        out_shape=[jax.ShapeDtypeStruct((bsz, seq, SSM_W), BF16),
                   jax.ShapeDtypeStruct((bsz, HEADS, HEAD_DIM, SSM_STATE), F32),
                   jax.ShapeDtypeStruct((bsz, CONV_W - 1, CONV_DIM), F32)],
        scratch_shapes=[pltpu.VMEM((ns, q + SUBLANES, CONV_DIM), F32),
                        pltpu.VMEM((ns, SSM_W, SSM_STATE), F32)],
        compiler_params=_cparams(("arbitrary", "arbitrary")),
        name="ssd_prompt",
    )(xbc.reshape(bsz, seq, CONV_DIM), dt_raw.reshape(bsz, seq, LANES), z.reshape(bsz, seq, SSM_W),
      conv_w, conv_b[None, :], _row128(dt_bias), _row128(a_log),
      jnp.repeat(d_skip.astype(F32), HEAD_DIM)[None, :], g_ssm[None, :], tri)
    return yg.reshape(bsz * seq, SSM_W), state, tail


def _ssd_dec_pre_body(xbc_ref, sc_ref, dt_ref, cw_ref, cb_ref, dtb_ref, alog_ref,
                      xs_ref, xdt_ref, bm_ref, cm_ref, dec_ref, cnew_ref):
    raw = xbc_ref[...]
    conv = raw * cw_ref[CONV_W - 1:CONV_W, :] + cb_ref[...]
    for i in range(CONV_W - 1):
        conv = conv + sc_ref[i] * cw_ref[i:i + 1, :]
    for i in range(CONV_W - 2):
        cnew_ref[i] = sc_ref[i + 1]
    cnew_ref[CONV_W - 2] = raw
    xc = conv * _sigmoid(conv)
    xs = xc[:, :SSM_W]
    xs_ref[...] = xs
    bm_ref[...] = xc[:, SSM_W:SSM_W + SSM_GROUPS * SSM_STATE]
    cm_ref[...] = xc[:, SSM_W + SSM_GROUPS * SSM_STATE:]
    dt = _softplus(dt_ref[...] + dtb_ref[...])
    dec_ref[...] = jnp.exp(dt * (-jnp.exp(alog_ref[...])))
    for h in range(HEADS):
        hs = slice(h * HEAD_DIM, (h + 1) * HEAD_DIM)
        xdt_ref[:, hs] = xs[:, hs] * dt[:, h:h + 1]


def _ssd_dec_pre(xbc, state_conv_l, dt_raw, conv_w, conv_b, dt_bias, a_log):
    nb = xbc.shape[0]
    gn = SSM_GROUPS * SSM_STATE
    return pl.pallas_call(
        _ssd_dec_pre_body,
        out_shape=[jax.ShapeDtypeStruct((nb, SSM_W), F32),
                   jax.ShapeDtypeStruct((nb, SSM_W), F32),
                   jax.ShapeDtypeStruct((nb, gn), F32),
                   jax.ShapeDtypeStruct((nb, gn), F32),
                   jax.ShapeDtypeStruct((nb, LANES), F32),
                   jax.ShapeDtypeStruct((CONV_W - 1, nb, CONV_DIM), F32)],
        compiler_params=pltpu.CompilerParams(vmem_limit_bytes=VMEM_LIMIT),
        name="ssd_decode_pre",
    )(xbc, state_conv_l, dt_raw, conv_w, conv_b[None, :], _row128(dt_bias), _row128(a_log))


SEQ_PER_STEP = 8


def _ssd_dec_state_body(dec_ref, xdt_ref, bm_ref, cm_ref, s_ref, snew_in_ref, snew_ref, y_ref):
    del snew_in_ref
    step = pl.program_id(0)
    seqs = range(SEQ_PER_STEP)
    xb = [jnp.broadcast_to(xdt_ref[0, :, i:i + 1], (SSM_W, SSM_STATE)) for i in seqs]
    hn = {}
    for i in seqs:
        b = step * SEQ_PER_STEP + i
        for h in range(HEADS):
            g = h // HEADS_PER_GROUP
            hs = slice(h * HEAD_DIM, (h + 1) * HEAD_DIM)
            brow = bm_ref[0, i:i + 1, g * SSM_STATE:(g + 1) * SSM_STATE]
            hn[i, h] = s_ref[i, h] * dec_ref[b, h] + xb[i][hs, :] * brow
            snew_ref[i, h] = hn[i, h]
    ys = {}
    for i in seqs:
        for h in range(HEADS):
            g = h // HEADS_PER_GROUP
            crow = cm_ref[0, i:i + 1, g * SSM_STATE:(g + 1) * SSM_STATE]
            ys[i, h] = jnp.sum(hn[i, h] * crow, axis=1, keepdims=True)
    for i in seqs:
        y_ref[0, :, i:i + 1] = jnp.concatenate([ys[i, h] for h in range(HEADS)], axis=0)


def _ssd_dec_state(layer, dec, xdt_t, bm3, cm3, state_all, new_all):
    nb = state_all.shape[1]
    steps = nb // SEQ_PER_STEP
    gn = SSM_GROUPS * SSM_STATE
    blk3 = lambda s: (s, 0, 0)
    blk5 = lambda s: (layer, s, 0, 0, 0)
    state_blk = pl.BlockSpec((None, SEQ_PER_STEP, HEADS, HEAD_DIM, SSM_STATE), blk5)
    return pl.pallas_call(
        _ssd_dec_state_body,
        grid=(steps,),
        in_specs=[pl.BlockSpec(memory_space=pltpu.SMEM),
                  pl.BlockSpec((1, SSM_W, SEQ_PER_STEP), blk3),
                  pl.BlockSpec((1, SEQ_PER_STEP, gn), blk3),
                  pl.BlockSpec((1, SEQ_PER_STEP, gn), blk3),
                  state_blk,
                  pl.BlockSpec(memory_space=pl.ANY)],
        out_specs=[state_blk, pl.BlockSpec((1, SSM_W, SEQ_PER_STEP), blk3)],
        out_shape=[jax.ShapeDtypeStruct(new_all.shape, F32),
                   jax.ShapeDtypeStruct((steps, SSM_W, SEQ_PER_STEP), F32)],
        input_output_aliases={5: 0},
        compiler_params=_cparams(("arbitrary",)),
        name="ssd_decode_state",
    )(dec, xdt_t, bm3, cm3, state_all, new_all)


def _gate_norm_body(y_ref, xs_ref, z_ref, dskip_ref, g_ref, o_ref):
    y = y_ref[...] + xs_ref[...] * dskip_ref[...]
    z = z_ref[...]
    y = y * (z * _sigmoid(z))
    y = y * lax.rsqrt(jnp.mean(y * y, axis=1, keepdims=True) + NORM_EPS) * g_ref[...]
    o_ref[...] = y.astype(BF16)


def _gate_norm(y, xs, z, d_skip, g_ssm):
    return pl.pallas_call(
        _gate_norm_body,
        out_shape=jax.ShapeDtypeStruct(y.shape, BF16),
        name="ssd_decode_gate",
    )(y, xs, z, jnp.repeat(d_skip.astype(F32), HEAD_DIM)[None, :], g_ssm[None, :])


def _layer_norm(u, g, b):
    mu = jnp.mean(u, axis=1, keepdims=True)
    d = u - mu
    var = jnp.mean(d * d, axis=1, keepdims=True)
    return d * lax.rsqrt(var + NORM_EPS) * g + b


def _outproj_body(att_ref, yg_ref, h_ref, wo_ref, gatt_ref, g1_ref, b1_ref, wr_ref, br_ref,
                  h1_ref, lg_ref):
    att = att_ref[...]
    att = att * lax.rsqrt(jnp.mean(att * att, axis=1, keepdims=True) + NORM_EPS) * gatt_ref[...]
    mix = _dot(att.astype(BF16), wo_ref[0:ATT_W, :]) + _dot(yg_ref[...], wo_ref[ATT_W:, :])
    h1 = _layer_norm(DN_ALPHA * h_ref[...] + mix, g1_ref[...], b1_ref[...])
    h1_ref[...] = h1
    lg_ref[...] = _dot_nt(h1.astype(BF16), wr_ref[...]) + br_ref[...]


def _outproj(att, yg, h, w_out_b, g_att, g1, b1, w_router_p, b_router_p, tm):
    t = att.shape[0]
    row = lambda i: (i, 0)
    const = lambda i: (0, 0)
    in_specs = [pl.BlockSpec((tm, ATT_W), row),
                pl.BlockSpec((tm, SSM_W), row),
                pl.BlockSpec((tm, D_MODEL), row),
                pl.BlockSpec((D_MODEL, D_MODEL), const),
                pl.BlockSpec((1, ATT_W), const),
                pl.BlockSpec((1, D_MODEL), const),
                pl.BlockSpec((1, D_MODEL), const),
                pl.BlockSpec((LANES, D_MODEL), const),
                pl.BlockSpec((1, LANES), const)]
    args = [att, yg, h, w_out_b, g_att[None, :], g1[None, :], b1[None, :], w_router_p, b_router_p]
    return pl.pallas_call(
        _outproj_body,
        grid=(t // tm,),
        in_specs=in_specs,
        out_specs=[pl.BlockSpec((tm, D_MODEL), row), pl.BlockSpec((tm, LANES), row)],
        out_shape=[jax.ShapeDtypeStruct((t, D_MODEL), F32),
                   jax.ShapeDtypeStruct((t, LANES), F32)],
        compiler_params=_cparams(("arbitrary",)),
        name="outproj",
    )(*args)


def _row_permute_body(tm, gather, idx_ref, *refs):
    if gather:
        src_ref, blk_ref, sem = refs
    else:
        blk_ref, _, dst_ref, sem = refs
    base = pl.program_id(0) * tm

    def copy(r):
        j = idx_ref[base + r]
        if gather:
            return pltpu.make_async_copy(src_ref.at[pl.ds(j, 1)], blk_ref.at[pl.ds(r, 1)], sem)
        return pltpu.make_async_copy(blk_ref.at[pl.ds(r, 1)], dst_ref.at[pl.ds(j, 1)], sem)

    def start(r, carry):
        copy(r).start()
        return carry

    def wait(r, carry):
        copy(r).wait()
        return carry

    lax.fori_loop(0, tm, start, 0, unroll=8)
    lax.fori_loop(0, tm, wait, 0, unroll=8)


def _row_permute(idx, src, dst, gather, tm):
    n_rows, d = idx.shape[0], src.shape[1]
    any_spec = pl.BlockSpec(memory_space=pl.ANY)
    blk_spec = pl.BlockSpec((tm, d), lambda i, idx: (i, 0))
    if gather:
        args, in_specs, out_specs, aliases = (idx, src), [any_spec], blk_spec, {}
        out_shape = jax.ShapeDtypeStruct((n_rows, d), src.dtype)
    else:
        args, in_specs, out_specs, aliases = (idx, src, dst), [blk_spec, any_spec], any_spec, {2: 0}
        out_shape = jax.ShapeDtypeStruct(dst.shape, dst.dtype)
    return pl.pallas_call(
        functools.partial(_row_permute_body, tm, gather),
        grid_spec=pltpu.PrefetchScalarGridSpec(
            num_scalar_prefetch=1,
            grid=(n_rows // tm,),
            in_specs=in_specs,
            out_specs=out_specs,
            scratch_shapes=[pltpu.SemaphoreType.DMA]),
        out_shape=out_shape,
        input_output_aliases=aliases,
        compiler_params=_cparams(("arbitrary",)),
        name="row_gather" if gather else "row_scatter",
    )(*args)


def _moe_body(lo_ref, hi_ref, first_ref, nact_ref,
              x_ref, wr_ref, br_ref, g2_ref, b2_ref,
              wg_lo, wu_lo, wd_lo, wg_hi, wu_hi, wd_hi,
              o_ref, wgb, wub, wdb):
    i = pl.program_id(0)

    @pl.when(i < nact_ref[0])
    def _():
        @pl.when(first_ref[i] == 1)
        def _():
            wgb[0] = wg_lo[...].astype(BF16)
            wub[0] = wu_lo[...].astype(BF16)
            wdb[0] = wd_lo[...].astype(BF16)
            wgb[1] = wg_hi[...].astype(BF16)
            wub[1] = wu_hi[...].astype(BF16)
            wdb[1] = wd_hi[...].astype(BF16)

        x = x_ref[...]
        xb = x.astype(BF16)
        logits = _dot_nt(xb, wr_ref[...]) + br_ref[...]
        lane = lax.broadcasted_iota(jnp.int32, logits.shape, 1)
        valid = lane < N_EXPERTS
        m = jnp.max(jnp.where(valid, logits, -jnp.inf), axis=1, keepdims=True)
        e = jnp.where(valid, jnp.exp(logits - m), 0.0)
        probs = e / jnp.sum(e, axis=1, keepdims=True)
        p_lo = jnp.sum(jnp.where(lane == lo_ref[i], probs, 0.0), axis=1, keepdims=True)
        p_hi = jnp.sum(jnp.where(lane == hi_ref[i], probs, 0.0), axis=1, keepdims=True)
        denom = p_lo + p_hi
        out = jnp.zeros_like(x)
        for slot, gate in ((0, p_lo / denom), (1, p_hi / denom)):
            hg = _dot(xb, wgb[slot])
            hu = _dot(xb, wub[slot])
            hmid = (hg * _sigmoid(hg)) * hu
            out = out + gate * _dot(hmid.astype(BF16), wdb[slot])
        o_ref[...] = _layer_norm(DN_ALPHA * x + out, g2_ref[...], b2_ref[...])

    @pl.when(i >= nact_ref[0])
    def _():
        o_ref[...] = jnp.zeros_like(o_ref)


def _moe_sorted(layer, x_sorted, tile_lo, tile_hi, tile_first, n_active,
                w_router_p, b_router_p, g2, b2, w_gate, w_up, w_down):
    n_tiles = tile_lo.shape[0]
    tm = TM_MOE

    def xmap(i, lo, hi, first, nact):
        return (jnp.minimum(i, nact[0] - 1), 0)

    const = lambda i, lo, hi, first, nact: (0, 0)
    w_lo = lambda i, lo, hi, first, nact: (layer, lo[i], 0, 0)
    w_hi = lambda i, lo, hi, first, nact: (layer, hi[i], 0, 0)
    up_blk = (None, None, D_MODEL, D_FF)
    dn_blk = (None, None, D_FF, D_MODEL)
    return pl.pallas_call(
        _moe_body,
        grid_spec=pltpu.PrefetchScalarGridSpec(
            num_scalar_prefetch=4,
            grid=(n_tiles,),
            in_specs=[pl.BlockSpec((tm, D_MODEL), xmap),
                      pl.BlockSpec((LANES, D_MODEL), const),
                      pl.BlockSpec((1, LANES), const),
                      pl.BlockSpec((1, D_MODEL), const),
                      pl.BlockSpec((1, D_MODEL), const),
                      pl.BlockSpec(up_blk, w_lo), pl.BlockSpec(up_blk, w_lo), pl.BlockSpec(dn_blk, w_lo),
                      pl.BlockSpec(up_blk, w_hi), pl.BlockSpec(up_blk, w_hi), pl.BlockSpec(dn_blk, w_hi)],
            out_specs=pl.BlockSpec((tm, D_MODEL), lambda i, lo, hi, first, nact: (i, 0)),
            scratch_shapes=[pltpu.VMEM((2, D_MODEL, D_FF), BF16),
                            pltpu.VMEM((2, D_MODEL, D_FF), BF16),
                            pltpu.VMEM((2, D_FF, D_MODEL), BF16)]),
        out_shape=jax.ShapeDtypeStruct(x_sorted.shape, F32),
        compiler_params=_cparams(("arbitrary",)),
        name="moe",
    )(tile_lo, tile_hi, tile_first, n_active,
      x_sorted, w_router_p, b_router_p, g2[None, :], b2[None, :],
      w_gate, w_up, w_down, w_gate, w_up, w_down)


def _route(logits, n_tiles):
    t = logits.shape[0]
    probs = jax.nn.softmax(logits, axis=-1)
    grouped = probs.reshape(t, N_EXPERT_GROUPS, EXPERTS_PER_GROUP)
    g_sel = jnp.argmax(grouped.max(-1), axis=-1)
    g_hot = (jnp.arange(N_EXPERT_GROUPS)[None, :] == g_sel[:, None])
    in_group = jnp.sum(jnp.where(g_hot[:, :, None], grouped, 0.0), axis=1)
    lane = jnp.arange(EXPERTS_PER_GROUP)[None, :]
    i1 = jnp.argmax(in_group, axis=-1)
    i2 = jnp.argmax(jnp.where(lane == i1[:, None], -jnp.inf, in_group), axis=-1)
    e_lo = jnp.minimum(i1, i2)
    e_hi = jnp.maximum(i1, i2)
    pair = e_lo * (2 * EXPERTS_PER_GROUP - 1 - e_lo) // 2 + (e_hi - e_lo - 1)
    bucket = (g_sel * N_PAIRS + pair).astype(jnp.int32)

    onehot = (bucket[:, None] == jnp.arange(N_BUCKETS, dtype=jnp.int32)[None, :]).astype(jnp.int32)
    rank = jnp.sum((jnp.cumsum(onehot, axis=0) - onehot) * onehot, axis=1)
    counts = jnp.sum(onehot, axis=0)
    tiles_b = (counts + TM_MOE - 1) // TM_MOE
    tile_end = jnp.cumsum(tiles_b)
    tile_start = tile_end - tiles_b
    pos = (jnp.sum(onehot * tile_start[None, :], axis=1) * TM_MOE + rank).astype(jnp.int32)
    n_active = tile_end[-1]

    tile_id = jnp.minimum(jnp.arange(n_tiles, dtype=jnp.int32), n_active - 1)
    tile_bucket = jnp.sum((tile_id[:, None] >= tile_end[None, :]).astype(jnp.int32), axis=1)
    grp = tile_bucket // N_PAIRS
    pr = tile_bucket % N_PAIRS
    pair_lo = (pr >= 3).astype(jnp.int32) + (pr >= 5).astype(jnp.int32)
    pair_hi = jnp.where(pair_lo == 0, pr + 1, jnp.where(pair_lo == 1, pr - 1, 3))
    tile_lo = (grp * EXPERTS_PER_GROUP + pair_lo).astype(jnp.int32)
    tile_hi = (grp * EXPERTS_PER_GROUP + pair_hi).astype(jnp.int32)
    prev = jnp.concatenate([jnp.full((1,), -1, jnp.int32), tile_bucket[:-1].astype(jnp.int32)])
    tile_first = (tile_bucket != prev).astype(jnp.int32)
    return pos, tile_lo, tile_hi, tile_first, n_active.astype(jnp.int32)[None]


def kernel(x_prompt, x_sample, cache_k, cache_v, state_ssm, state_conv, page_table, w_in, att_bias,
           conv_w, conv_b, dt_bias, a_log, d_skip, attn_norm_g, ssm_norm_g, w_out, ln1_g, ln1_b,
           ln2_g, ln2_b, w_router, b_router, w_gate, w_up, w_down):
    bsz, seq, _ = x_prompt.shape
    nb = x_sample.shape[0]
    tp = bsz * seq
    t_all = tp + nb
    n_tiles = t_all // TM_MOE + N_BUCKETS
    pool, page = cache_k.shape[1], cache_k.shape[2]

    hp = x_prompt.reshape(tp, D_MODEL)
    hs = x_sample.reshape(nb, D_MODEL)
    cache_k4 = cache_k.transpose(0, 1, 3, 4, 2).reshape(DEPTH, pool, ATT_W, page)
    cache_v4 = cache_v.transpose(0, 1, 3, 4, 2).reshape(DEPTH, pool, ATT_W, page)
    state_conv_t = state_conv.transpose(0, 2, 1, 3)
    w_in_t = w_in.transpose(0, 2, 1).astype(BF16)
    w_main_t = w_in_t[:, :IN_MAIN, :]
    w_dt_t = jnp.pad(w_in_t[:, IN_MAIN:, :], ((0, 0), (0, LANES - HEADS), (0, 0)))
    w_out_b = w_out.astype(BF16)
    w_router_p = jnp.pad(w_router.T.astype(BF16), ((0, LANES - N_EXPERTS), (0, 0)))
    b_router_p = jnp.pad(b_router.astype(F32), (0, LANES - N_EXPERTS))[None, :]
    head_of_row = jnp.arange(2 * HEADS)[:, None]
    head_of_lane = jnp.arange(ATT_W)[None, :] // HEAD_DIM

    k_p = jnp.zeros((DEPTH, bsz, ATT_W, seq), F32)
    v_p = jnp.zeros((DEPTH, bsz, ATT_W, seq), F32)
    k_s = jnp.zeros((DEPTH, 1, ATT_W, nb), F32)
    v_s = jnp.zeros((DEPTH, 1, ATT_W, nb), F32)

    ssm_s = jnp.zeros(state_ssm.shape, F32)

    outs = [[] for _ in range(3)]
    for l in range(DEPTH):
        q8, kb, vb, k_p, v_p, z_p, xbc_p, dt_p = _inproj(
            l, hp, w_main_t[l], w_dt_t[l], k_p, v_p, bsz, TM_PROJ)
        q8s, _, _, k_s, v_s, z_s, xbc_s, dt_s = _inproj(
            l, hs, w_main_t[l], w_dt_t[l], k_s, v_s, 1, nb)
        q_rows = q8s.transpose(1, 0, 2).reshape(nb, 1, ATT_W)
        q_bd = jnp.where(head_of_row[None] == head_of_lane[None], q_rows, 0).astype(BF16)
        bias_b = jnp.broadcast_to(
            jnp.pad(att_bias[l].astype(F32), (0, HEADS))[:, None], (2 * HEADS, page))
        att_p, att_s = _attention(l, q8, kb, vb, att_bias[l].astype(F32), bsz, seq,
                                  q_bd, bias_b, cache_k4, cache_v4, page_table)
        att_s = att_s.reshape(nb, ATT_W)
        yg_p, ssm_p, conv_p = _ssd_prompt(xbc_p, dt_p, z_p, conv_w[l], conv_b[l], dt_bias[l], a_log[l],
                                          d_skip[l], ssm_norm_g[l], bsz, seq)
        xs_s, xdt_s, bm_s, cm_s, dec_s, conv_s = _ssd_dec_pre(
            xbc_s, state_conv_t[l], dt_s, conv_w[l], conv_b[l], dt_bias[l], a_log[l])
        steps = nb // SEQ_PER_STEP
        xdt_t = xdt_s.reshape(steps, SEQ_PER_STEP, SSM_W).transpose(0, 2, 1)
        ssm_s, y_t = _ssd_dec_state(l, dec_s, xdt_t,
                                    bm_s.reshape(steps, SEQ_PER_STEP, -1),
                                    cm_s.reshape(steps, SEQ_PER_STEP, -1), state_ssm, ssm_s)
        y_s = y_t.transpose(0, 2, 1).reshape(nb, SSM_W)
        yg_s = _gate_norm(y_s, xs_s, z_s, d_skip[l], ssm_norm_g[l])
        h1_p, lg_p = _outproj(att_p, yg_p, hp, w_out_b[l], attn_norm_g[l], ln1_g[l], ln1_b[l],
                              w_router_p, b_router_p, TM_PROJ)
        h1_s, lg_s = _outproj(att_s, yg_s, hs, w_out_b[l], attn_norm_g[l], ln1_g[l], ln1_b[l],
                              w_router_p, b_router_p, nb)
        logits = jnp.concatenate([lg_p[:, :N_EXPERTS], lg_s[:, :N_EXPERTS]], axis=0)
        pos, tile_lo, tile_hi, tile_first, n_active = _route(logits, n_tiles)
        x_sorted = jnp.zeros((n_tiles * TM_MOE, D_MODEL), F32)
        x_sorted = _row_permute(pos[:tp], h1_p, x_sorted, False, min(TM_PERM, tp))
        x_sorted = _row_permute(pos[tp:], h1_s, x_sorted, False, nb)
        y_sorted = _moe_sorted(l, x_sorted, tile_lo, tile_hi, tile_first, n_active,
                               w_router_p, b_router_p, ln2_g[l], ln2_b[l], w_gate, w_up, w_down)
        hp = _row_permute(pos[:tp], y_sorted, None, True, min(TM_PERM, tp))
        hs = _row_permute(pos[tp:], y_sorted, None, True, nb)

        for lst, val in zip(outs, (ssm_p, conv_p, conv_s)):
            lst.append(val)

    ssm_p, conv_p, conv_s = [jnp.stack(o) for o in outs]

    def untranspose(kt, n, length):
        return kt.reshape(DEPTH, n, HEADS, HEAD_DIM, length).transpose(0, 1, 4, 2, 3)

    return (hp.reshape(bsz, seq, D_MODEL),
            hs.reshape(nb, 1, D_MODEL),
            untranspose(k_p, bsz, seq),
            untranspose(v_p, bsz, seq),
            ssm_p,
            conv_p,
            untranspose(k_s, 1, nb).transpose(0, 2, 1, 3, 4),
            untranspose(v_s, 1, nb).transpose(0, 2, 1, 3, 4),
            ssm_s,
            conv_s.transpose(0, 2, 1, 3))
```
